```python
import math
import jax, jax.numpy as jnp
from jax import lax
import numpy as np

D_MODEL = 2048
BATCH = 16
SEQ = 2048
DEPTH = 1
DEC_BATCH = 128
DEC_SEQ = 1
PAST_LEN = 16384
PAGE_SIZE = 128

SWA_HEADS = 16
SWA_KV_HEADS = 4
SWA_GROUP = SWA_HEADS // SWA_KV_HEADS
SWA_HEAD_DIM = 64
WINDOW = 128
SWA_BLOCK = 128
SWA_Q_DIM = SWA_HEADS * SWA_HEAD_DIM
SWA_KV_DIM = SWA_KV_HEADS * SWA_HEAD_DIM

SSD_D_INNER = D_MODEL
SSD_HEAD_DIM = 64
SSD_HEADS = SSD_D_INNER // SSD_HEAD_DIM
SSD_GROUPS = 4
SSD_HPG = SSD_HEADS // SSD_GROUPS
SSD_D_STATE = 128
SSD_CONV = 4
SSD_CHUNK = 128
SSD_CONV_CH = SSD_D_INNER + 2 * SSD_GROUPS * SSD_D_STATE

MEM_TOKENS = 256
MEM_HEADS = 4
MEM_HEAD_DIM = 256
MEM_DIM = MEM_HEADS * MEM_HEAD_DIM

N_BRANCH = 3
D_FF = ((8 * D_MODEL + 3 * 256 - 1) // (3 * 256)) * 256
EPS = 1e-6

IN_SPLITS = (SWA_Q_DIM, SWA_KV_DIM, SWA_KV_DIM, SSD_D_INNER, SSD_CONV_CH, SSD_HEADS, MEM_DIM, N_BRANCH * D_MODEL)
IN_DIM = sum(IN_SPLITS)

kernel_name = "hybrid_swa_ssd_memxattn_decoder_step"


def _split_points(sizes):
    pts, acc = [], 0
    for s in sizes[:-1]:
        acc += s
        pts.append(acc)
    return pts


def _rmsnorm(x, g):
    xf = x.astype(jnp.float32)
    y = xf * lax.rsqrt(jnp.mean(jnp.square(xf), axis=-1, keepdims=True) + EPS)
    return (y * g.astype(jnp.float32)).astype(x.dtype)


def _alibi_slopes():
    h = jnp.arange(1, SWA_HEADS + 1, dtype=jnp.float32)
    return jnp.exp2(-8.0 * h / SWA_HEADS).reshape(SWA_KV_HEADS, SWA_GROUP)


def _in_proj(x, norm_mix, w_in, q_norm_swa, k_norm_swa, q_norm_mem):
    b, t, _ = x.shape
    h = _rmsnorm(x, norm_mix)
    q, k, v, z, xbc, dt, qm, g = jnp.split(h @ w_in, _split_points(IN_SPLITS), axis=-1)
    q = _rmsnorm(q.reshape(b, t, SWA_KV_HEADS, SWA_GROUP, SWA_HEAD_DIM), q_norm_swa)
    k = _rmsnorm(k.reshape(b, t, SWA_KV_HEADS, SWA_HEAD_DIM), k_norm_swa)
    v = v.reshape(b, t, SWA_KV_HEADS, SWA_HEAD_DIM)
    qm = _rmsnorm(qm.reshape(b, t, MEM_HEADS, MEM_HEAD_DIM), q_norm_mem)
    gates = jax.nn.sigmoid(g.astype(jnp.float32)).astype(x.dtype).reshape(b, t, N_BRANCH, D_MODEL)
    return q, k, v, z, xbc, dt, qm, gates


def _swa_attend(q, k, v, q_pos, k_pos, sinks):
    f32 = jnp.float32
    s = jnp.einsum('...qhgd,...khd->...hgqk', q, k).astype(f32) * (SWA_HEAD_DIM ** -0.5)
    dist = q_pos[..., :, None] - k_pos[..., None, :]
    allowed = (dist >= 0) & (dist <= WINDOW) & (k_pos[..., None, :] >= 0)
    dist = dist[..., None, None, :, :].astype(f32)
    allowed = allowed[..., None, None, :, :]
    s = jnp.where(allowed, s - _alibi_slopes()[:, :, None, None] * dist, -jnp.inf)
    sink = sinks.astype(f32).reshape(SWA_KV_HEADS, SWA_GROUP)[:, :, None]
    m = jnp.maximum(jnp.max(s, axis=-1), sink)
    p = jnp.exp(s - m[..., None])
    denom = jnp.sum(p, axis=-1) + jnp.exp(sink - m)
    p = p / denom[..., None]
    o = jnp.einsum('...hgqk,...khd->...qhgd', p, v.astype(f32))
    return o.astype(q.dtype)


def _swa_prompt(q, k, v, sinks):
    b, t = q.shape[:2]
    nb = t // SWA_BLOCK
    qb = q.reshape(b, nb, SWA_BLOCK, SWA_KV_HEADS, SWA_GROUP, SWA_HEAD_DIM)
    kb = k.reshape(b, nb, SWA_BLOCK, SWA_KV_HEADS, SWA_HEAD_DIM)
    vb = v.reshape(b, nb, SWA_BLOCK, SWA_KV_HEADS, SWA_HEAD_DIM)
    pad = ((0, 0), (1, 0), (0, 0), (0, 0), (0, 0))
    kcat = jnp.concatenate([jnp.pad(kb, pad)[:, :-1], kb], axis=2)
    vcat = jnp.concatenate([jnp.pad(vb, pad)[:, :-1], vb], axis=2)
    pos = jnp.arange(t, dtype=jnp.int32).reshape(nb, SWA_BLOCK)
    kpos = jnp.concatenate([pos - SWA_BLOCK, pos], axis=-1)
    o = _swa_attend(qb, kcat, vcat, pos, kpos, sinks)
    return o.reshape(b, t, SWA_Q_DIM)


def _ssd(xbc_raw, z, dt_raw, conv_buf, h0, conv_w, conv_b, dt_bias, a_log, d_skip, ssd_norm):
    f32 = jnp.float32
    b, L, _ = xbc_raw.shape
    xpad = jnp.concatenate([conv_buf.astype(f32), xbc_raw.astype(f32)], axis=1)
    conv = lax.conv_general_dilated(xpad, conv_w.astype(f32)[:, None, :], (1,), 'VALID',
                                    dimension_numbers=('NWC', 'WIO', 'NWC'),
                                    feature_group_count=SSD_CONV_CH)
    xbc = jax.nn.silu(conv + conv_b.astype(f32))
    new_conv = xpad[:, L:]
    xs, bm, cm = jnp.split(xbc, [SSD_D_INNER, SSD_D_INNER + SSD_GROUPS * SSD_D_STATE], axis=-1)
    l = SSD_CHUNK if L % SSD_CHUNK == 0 else L
    c = L // l
    x = xs.reshape(b, c, l, SSD_GROUPS, SSD_HPG, SSD_HEAD_DIM)
    bm = bm.reshape(b, c, l, SSD_GROUPS, SSD_D_STATE)
    cm = cm.reshape(b, c, l, SSD_GROUPS, SSD_D_STATE)
    dt = jax.nn.softplus(dt_raw.astype(f32) + dt_bias.astype(f32)).reshape(b, c, l, SSD_GROUPS, SSD_HPG)
    a = -jnp.exp(a_log.astype(f32)).reshape(SSD_GROUPS, SSD_HPG)
    acs = jnp.cumsum(dt * a, axis=2)
    acs_t = jnp.moveaxis(acs, 2, -1)
    dt_t = jnp.moveaxis(dt, 2, -1)
    causal = jnp.tril(jnp.ones((l, l), dtype=bool))
    decay = jnp.exp(jnp.where(causal, acs_t[..., :, None] - acs_t[..., None, :], -jnp.inf))
    cb = jnp.einsum('bclgn,bcsgn->bcgls', cm, bm)
    w_intra = cb[:, :, :, None] * decay * dt_t[..., None, :]
    y_diag = jnp.einsum('bcgrls,bcsgrp->bclgrp', w_intra, x)
    xw = x * (jnp.exp(acs[:, :, -1:] - acs) * dt)[..., None]
    states = jnp.einsum('bclgn,bclgrp->bcgrpn', bm, xw)
    chunk_decay = jnp.exp(acs[:, :, -1])

    def step(h, inp):
        dec, st = inp
        return dec[..., None, None] * h + st, h

    h_init = h0.astype(f32).reshape(b, SSD_GROUPS, SSD_HPG, SSD_HEAD_DIM, SSD_D_STATE)
    h_last, h_prev = lax.scan(step, h_init, (jnp.moveaxis(chunk_decay, 1, 0), jnp.moveaxis(states, 1, 0)))
    h_prev = jnp.moveaxis(h_prev, 0, 1)
    y_off = jnp.einsum('bclgn,bcgrpn->bclgrp', cm, h_prev) * jnp.exp(acs)[..., None]
    y = y_diag + y_off + d_skip.astype(f32).reshape(SSD_GROUPS, SSD_HPG, 1) * x
    gsz = SSD_D_INNER // SSD_GROUPS
    y = y.reshape(b, L, SSD_GROUPS, gsz) * jax.nn.silu(z.astype(f32)).reshape(b, L, SSD_GROUPS, gsz)
    y = y * lax.rsqrt(jnp.mean(jnp.square(y), axis=-1, keepdims=True) + EPS)
    y = y.reshape(b, L, SSD_D_INNER) * ssd_norm.astype(f32)
    h_out = h_last.reshape(b, SSD_HEADS, SSD_HEAD_DIM, SSD_D_STATE)
    return y.astype(z.dtype), new_conv.astype(conv_buf.dtype), h_out.astype(h0.dtype)


def _mem_kv(mem, norm_mem, w_mem_kv, k_norm_mem):
    b, m, _ = mem.shape
    k, v = jnp.split(_rmsnorm(mem, norm_mem) @ w_mem_kv, 2, axis=-1)
    k = _rmsnorm(k.reshape(b, m, MEM_HEADS, MEM_HEAD_DIM), k_norm_mem)
    return k, v.reshape(b, m, MEM_HEADS, MEM_HEAD_DIM)


def _mem_attend(q, k, v):
    b, t = q.shape[:2]
    s = jnp.einsum('bthd,bmhd->bhtm', q, k).astype(jnp.float32) * (MEM_HEAD_DIM ** -0.5)
    p = jax.nn.softmax(s, axis=-1)
    o = jnp.einsum('bhtm,bmhd->bthd', p, v.astype(jnp.float32))
    return o.reshape(b, t, MEM_DIM).astype(q.dtype)


def _merge_ffn(x, a_out, s_out, m_out, gates, w_up_swa, w_up_ssd, w_up_mem, w_out, norm_ffn, w_gate, w_up, w_down):
    merged = (gates[:, :, 0] * (a_out @ w_up_swa)
              + gates[:, :, 1] * (s_out @ w_up_ssd)
              + gates[:, :, 2] * (m_out @ w_up_mem))
    x = x + merged @ w_out
    h = _rmsnorm(x, norm_ffn)
    return x + (jax.nn.silu(h @ w_gate) * (h @ w_up)) @ w_down


def setup_inputs(seed: int = 0) -> dict:
    key = jax.random.key(seed)
    ks = iter(jax.random.split(key, 48))
    f32 = jnp.float32

    def nrm(shape, scale=1.0):
        return jax.random.normal(next(ks), shape, f32) * scale

    def gain(n):
        return 1.0 + nrm((DEPTH, n), 0.02)

    w_buf = min(WINDOW, PAST_LEN)
    dt0 = jnp.exp(jax.random.uniform(next(ks), (DEPTH, SSD_HEADS), f32, math.log(1e-3), math.log(1e-1)))
    dt_bias = dt0 + jnp.log(-jnp.expm1(-dt0))
    a_log = jnp.log(jax.random.uniform(next(ks), (DEPTH, SSD_HEADS), f32, 1.0, 16.0))
    return {
        "x_prompt": nrm((BATCH, SEQ, D_MODEL)),
        "x_sample": nrm((DEC_BATCH, DEC_SEQ, D_MODEL)),
        "cache_swa_k": nrm((DEPTH, DEC_BATCH, w_buf, SWA_KV_HEADS, SWA_HEAD_DIM)),
        "cache_swa_v": nrm((DEPTH, DEC_BATCH, w_buf, SWA_KV_HEADS, SWA_HEAD_DIM)),
        "cache_mem_k": nrm((DEPTH, DEC_BATCH, MEM_TOKENS, MEM_HEADS, MEM_HEAD_DIM)),
        "cache_mem_v": nrm((DEPTH, DEC_BATCH, MEM_TOKENS, MEM_HEADS, MEM_HEAD_DIM)),
        "state_ssm": nrm((DEPTH, DEC_BATCH, SSD_HEADS, SSD_HEAD_DIM, SSD_D_STATE), 0.1),
        "state_conv": nrm((DEPTH, DEC_BATCH, SSD_CONV - 1, SSD_CONV_CH)),
        "mem_prompt": nrm((BATCH, MEM_TOKENS, D_MODEL)),
        "norm_mix": gain(D_MODEL),
        "w_in": nrm((DEPTH, D_MODEL, IN_DIM), D_MODEL ** -0.5),
        "q_norm_swa": gain(SWA_HEAD_DIM),
        "k_norm_swa": gain(SWA_HEAD_DIM),
        "swa_sinks": nrm((DEPTH, SWA_HEADS), 0.5),
        "conv_w": nrm((DEPTH, SSD_CONV, SSD_CONV_CH), SSD_CONV ** -0.5),
        "conv_b": nrm((DEPTH, SSD_CONV_CH), 0.01),
        "dt_bias": dt_bias,
        "a_log": a_log,
        "d_skip": 1.0 + nrm((DEPTH, SSD_HEADS), 0.02),
        "ssd_norm": gain(SSD_D_INNER),
        "norm_mem": gain(D_MODEL),
        "w_mem_kv": nrm((DEPTH, D_MODEL, 2 * MEM_DIM), D_MODEL ** -0.5),
        "q_norm_mem": gain(MEM_HEAD_DIM),
        "k_norm_mem": gain(MEM_HEAD_DIM),
        "w_up_swa": nrm((DEPTH, SWA_Q_DIM, D_MODEL), SWA_Q_DIM ** -0.5),
        "w_up_ssd": nrm((DEPTH, SSD_D_INNER, D_MODEL), SSD_D_INNER ** -0.5),
        "w_up_mem": nrm((DEPTH, MEM_DIM, D_MODEL), MEM_DIM ** -0.5),
        "w_out": nrm((DEPTH, D_MODEL, D_MODEL), D_MODEL ** -0.5),
        "norm_ffn": gain(D_MODEL),
        "w_gate": nrm((DEPTH, D_MODEL, D_FF), D_MODEL ** -0.5),
        "w_up": nrm((DEPTH, D_MODEL, D_FF), D_MODEL ** -0.5),
        "w_down": nrm((DEPTH, D_FF, D_MODEL), D_FF ** -0.5),
    }


def reference(x_prompt, x_sample, cache_swa_k, cache_swa_v, cache_mem_k, cache_mem_v, state_ssm, state_conv,
              mem_prompt, norm_mix, w_in, q_norm_swa, k_norm_swa, swa_sinks, conv_w, conv_b, dt_bias, a_log,
              d_skip, ssd_norm, norm_mem, w_mem_kv, q_norm_mem, k_norm_mem, w_up_swa, w_up_ssd, w_up_mem,
              w_out, norm_ffn, w_gate, w_up, w_down):
    bp, tp, _ = x_prompt.shape
    bs, ts, _ = x_sample.shape
    w_buf = cache_swa_k.shape[2]
    w_p = min(WINDOW, tp)
    yp, ys = x_prompt, x_sample
    p_k, p_v, p_mk, p_mv, p_h, p_c = [], [], [], [], [], []
    s_k, s_v, s_h, s_c = [], [], [], []
    for l in range(DEPTH):
        ssd_w = (conv_w[l], conv_b[l], dt_bias[l], a_log[l], d_skip[l], ssd_norm[l])
        out_w = (w_up_swa[l], w_up_ssd[l], w_up_mem[l], w_out[l], norm_ffn[l], w_gate[l], w_up[l], w_down[l])

        q, k, v, z, xbc, dt, qm, gates = _in_proj(yp, norm_mix[l], w_in[l], q_norm_swa[l], k_norm_swa[l], q_norm_mem[l])
        a_out = _swa_prompt(q, k, v, swa_sinks[l])
        conv0 = jnp.zeros((bp, SSD_CONV - 1, SSD_CONV_CH), state_conv.dtype)
        h0 = jnp.zeros((bp, SSD_HEADS, SSD_HEAD_DIM, SSD_D_STATE), state_ssm.dtype)
        s_out, conv_new, h_new = _ssd(xbc, z, dt, conv0, h0, *ssd_w)
        mk, mv = _mem_kv(mem_prompt, norm_mem[l], w_mem_kv[l], k_norm_mem[l])
        m_out = _mem_attend(qm, mk, mv)
        yp = _merge_ffn(yp, a_out, s_out, m_out, gates, *out_w)
        p_k.append(k[:, tp - w_p:])
        p_v.append(v[:, tp - w_p:])
        p_mk.append(mk)
        p_mv.append(mv)
        p_h.append(h_new)
        p_c.append(conv_new)

        q, k, v, z, xbc, dt, qm, gates = _in_proj(ys, norm_mix[l], w_in[l], q_norm_swa[l], k_norm_swa[l], q_norm_mem[l])
        k_all = jnp.concatenate([cache_swa_k[l].astype(k.dtype), k], axis=1)
        v_all = jnp.concatenate([cache_swa_v[l].astype(v.dtype), v], axis=1)
        k_pos = PAST_LEN - w_buf + jnp.arange(w_buf + ts, dtype=jnp.int32)
        q_pos = PAST_LEN + jnp.arange(ts, dtype=jnp.int32)
        a_out = _swa_attend(q, k_all, v_all, q_pos, k_pos, swa_sinks[l]).reshape(bs, ts, SWA_Q_DIM)
        s_out, conv_new, h_new = _ssd(xbc, z, dt, state_conv[l], state_ssm[l], *ssd_w)
        m_out = _mem_attend(qm, cache_mem_k[l], cache_mem_v[l])
        ys = _merge_ffn(ys, a_out, s_out, m_out, gates, *out_w)
        s_k.append(k_all[:, ts:])
        s_v.append(v_all[:, ts:])
        s_h.append(h_new)
        s_c.append(conv_new)

    return (yp, ys,
            jnp.stack(p_k), jnp.stack(p_v), jnp.stack(p_mk), jnp.stack(p_mv), jnp.stack(p_h), jnp.stack(p_c),
            jnp.stack(s_k), jnp.stack(s_v), jnp.stack(s_h), jnp.stack(s_c))
```

```python
import functools

import jax
import jax.numpy as jnp
from jax import lax
from jax.experimental import pallas as pl
from jax.experimental.pallas import tpu as pltpu

f32 = jnp.float32
bf16 = jnp.bfloat16
SDS = jax.ShapeDtypeStruct

LANES = 128
SUBLANES = 8
VMEM_BYTES_V7X = 64 * 1024 * 1024
VMEM_HEADROOM = 8 * 1024 * 1024

EPS = 1e-6
SWA_HEADS = 16
SWA_KV_HEADS = 4
SWA_GROUP = SWA_HEADS // SWA_KV_HEADS
SWA_HEAD_DIM = 64
WINDOW = 128
SWA_BLOCK = 128
SSD_HEAD_DIM = 64
SSD_HEADS = 32
SSD_GROUPS = 4
SSD_D_STATE = 128
SSD_CONV = 4
SSD_CHUNK = 128
MEM_HEADS = 4
MEM_HEAD_DIM = 256
N_BRANCH = 3
MASKED = -1e30

NT_DIMS = (((1,), (1,)), ((), ()))
TN_DIMS = (((0,), (0,)), ((), ()))


def _params(semantics, block_bytes):
    limit = min(int(block_bytes) + VMEM_HEADROOM, VMEM_BYTES_V7X - VMEM_HEADROOM)
    return pltpu.CompilerParams(dimension_semantics=semantics, vmem_limit_bytes=limit)


def _nbytes(shape, dtype):
    n = 1
    for s in shape:
        n *= s
    return n * jnp.dtype(dtype).itemsize


def _resident(shape):
    nd = len(shape)
    return pl.BlockSpec(shape, lambda *_: (0,) * nd, pipeline_mode=pl.Buffered(1))


def _sigmoid(x):
    return 1.0 / (1.0 + jnp.exp(-x))


def _softplus(x):
    return jnp.maximum(x, 0.0) + jnp.log1p(jnp.exp(-jnp.abs(x)))


def _split_dot(v, m):
    hi = v.astype(bf16)
    lo = (v - hi.astype(f32)).astype(bf16)
    return jnp.dot(hi, m, preferred_element_type=f32) + jnp.dot(lo, m, preferred_element_type=f32)


def _headnorm64(acc, gain, blockdiag):
    outs = []
    for c in range(acc.shape[1] // LANES):
        a = acc[:, c * LANES:(c + 1) * LANES]
        ss = _split_dot(a * a, blockdiag)
        outs.append(a * lax.rsqrt(ss * (1.0 / SWA_HEAD_DIM) + EPS) * gain[:, c * LANES:(c + 1) * LANES])
    return outs[0] if len(outs) == 1 else jnp.concatenate(outs, axis=-1)


def _norm_body(x_ref, g_ref, o_ref):
    x = x_ref[...]
    ms = jnp.mean(x * x, axis=-1, keepdims=True)
    o_ref[...] = (x * lax.rsqrt(ms + EPS) * g_ref[...]).astype(o_ref.dtype)


def _rmsnorm(x, gain, tm):
    m, d = x.shape
    est = 2 * (_nbytes((tm, d), f32) + _nbytes((tm, d), bf16)) + _nbytes((tm, d), f32)
    return pl.pallas_call(
        _norm_body, out_shape=SDS((m, d), bf16), grid=(m // tm,),
        in_specs=[pl.BlockSpec((tm, d), lambda i: (i, 0)), _resident((1, d))],
        out_specs=pl.BlockSpec((tm, d), lambda i: (i, 0)),
        compiler_params=_params(("parallel",), est), name="rmsnorm")(x, gain.reshape(1, d))


def _qkv_body(h_ref, w_ref, gq_ref, gk_ref, bd_ref, q_ref, k_ref, v_ref, dt_ref, *last_refs, nq, nk, nv, chunk):
    h = h_ref[...]
    bd = bd_ref[...]
    tm = h.shape[0]
    for c0 in range(0, nq, chunk):
        acc = jnp.dot(h, w_ref[:, c0:c0 + chunk], preferred_element_type=f32)
        q_ref[:, c0:c0 + chunk] = _headnorm64(acc, gq_ref[:, c0:c0 + chunk], bd).astype(q_ref.dtype)
    for c0 in range(0, nk, chunk):
        w = min(chunk, nk - c0)
        acc = jnp.dot(h, w_ref[:, nq + c0:nq + c0 + w], preferred_element_type=f32)
        kn = _headnorm64(acc, gk_ref[:, c0:c0 + w], bd)
        k_ref[:, c0:c0 + w] = kn.astype(k_ref.dtype)
        if last_refs:
            last_refs[0][:, c0:c0 + w] = kn[tm - WINDOW:, :]
    for c0 in range(0, nv, chunk):
        w = min(chunk, nv - c0)
        acc = jnp.dot(h, w_ref[:, nq + nk + c0:nq + nk + c0 + w], preferred_element_type=f32)
        v_ref[:, c0:c0 + w] = acc.astype(v_ref.dtype)
        if last_refs:
            last_refs[1][:, c0:c0 + w] = acc[tm - WINDOW:, :]
    dt_ref[...] = jnp.dot(h, w_ref[:, nq + nk + nv:], preferred_element_type=f32)


def _qkv_proj(h, w, gq, gk, blockdiag, nq, nk, nv, tm, tiles_per_batch, q_dtype, kv_dtype, want_last):
    m, d = h.shape
    n = w.shape[1]
    chunk = 512
    out_shape = [SDS((m, nq), q_dtype), SDS((m, nk), kv_dtype), SDS((m, nv), kv_dtype), SDS((m, LANES), f32)]
    out_specs = [pl.BlockSpec((tm, nq), lambda i: (i, 0)), pl.BlockSpec((tm, nk), lambda i: (i, 0)),
                 pl.BlockSpec((tm, nv), lambda i: (i, 0)), pl.BlockSpec((tm, LANES), lambda i: (i, 0))]
    if want_last:
        nb = m // (tm * tiles_per_batch)
        out_shape += [SDS((nb * WINDOW, nk), f32), SDS((nb * WINDOW, nv), f32)]
        out_specs += [pl.BlockSpec((WINDOW, nk), lambda i: (i // tiles_per_batch, 0)),
                      pl.BlockSpec((WINDOW, nv), lambda i: (i // tiles_per_batch, 0))]
    est = (2 * _nbytes((tm, d), bf16) + _nbytes((d, n), bf16) + 2 * _nbytes((tm, nq + nk + nv), f32)
           + 4 * _nbytes((tm, chunk), f32))
    body = functools.partial(_qkv_body, nq=nq, nk=nk, nv=nv, chunk=chunk)
    return pl.pallas_call(
        body, out_shape=out_shape, grid=(m // tm,),
        in_specs=[pl.BlockSpec((tm, d), lambda i: (i, 0)), _resident((d, n)), _resident((1, nq)),
                  _resident((1, nk)), _resident((LANES, LANES))],
        out_specs=out_specs, compiler_params=_params(("arbitrary",), est), name="qkv_proj")(h, w, gq, gk, blockdiag)


def _proj_body(h_ref, w_ref, *refs, epilogue, chunk, n_aux):
    aux, outs = refs[:n_aux], refs[n_aux:]
    h = h_ref[...]
    tn = w_ref.shape[1]
    for c0 in range(0, tn, chunk):
        acc = jnp.dot(h, w_ref[:, c0:c0 + chunk], preferred_element_type=f32)
        epilogue(acc, c0, chunk, aux, outs)


def _epi_plain(acc, c0, w, aux, outs):
    outs[0][:, c0:c0 + w] = acc.astype(outs[0].dtype)


def _epi_plain_tail(acc, c0, w, aux, outs):
    outs[0][:, c0:c0 + w] = acc.astype(outs[0].dtype)
    outs[1][:, c0:c0 + w] = acc[acc.shape[0] - SUBLANES:, :]


def _epi_sigmoid(acc, c0, w, aux, outs):
    outs[0][:, c0:c0 + w] = _sigmoid(acc).astype(outs[0].dtype)


def _epi_headnorm256(acc, c0, w, aux, outs):
    gain = aux[0]
    for h0 in range(0, w, MEM_HEAD_DIM):
        a = acc[:, h0:h0 + MEM_HEAD_DIM]
        ms = jnp.mean(a * a, axis=-1, keepdims=True)
        y = a * lax.rsqrt(ms + EPS) * gain[:, c0 + h0:c0 + h0 + MEM_HEAD_DIM]
        outs[0][:, c0 + h0:c0 + h0 + MEM_HEAD_DIM] = y.astype(outs[0].dtype)


def _epi_residual(acc, c0, w, aux, outs):
    outs[0][:, c0:c0 + w] = aux[0][:, c0:c0 + w] + acc


def _proj(h, w, epilogue, out_dtype, tm, tn, aux=(), aux_specs=(), tail_rows_per=None, chunk=512, name="proj"):
    m, d = h.shape
    n = w.shape[1]
    nj = n // tn
    chunk = min(chunk, tn)
    w_spec = _resident((d, n)) if nj == 1 else pl.BlockSpec((d, tn), lambda i, j: (0, j))
    out_shape = [SDS((m, n), out_dtype)]
    out_specs = [pl.BlockSpec((tm, tn), lambda i, j: (i, j))]
    if tail_rows_per is not None:
        nb = m // (tm * tail_rows_per)
        out_shape.append(SDS((nb * SUBLANES, n), f32))
        out_specs.append(pl.BlockSpec((SUBLANES, tn), lambda i, j: (i // tail_rows_per, j)))
    est = (2 * _nbytes((tm, d), bf16) + (1 if nj == 1 else 2) * _nbytes((d, tn), bf16)
           + 2 * _nbytes((tm, tn), out_dtype) + 4 * _nbytes((tm, chunk), f32)
           + sum(2 * _nbytes(s.block_shape, f32) for s in aux_specs))
    body = functools.partial(_proj_body, epilogue=epilogue, chunk=chunk, n_aux=len(aux))
    res = pl.pallas_call(
        body, out_shape=out_shape, grid=(m // tm, nj),
        in_specs=[pl.BlockSpec((tm, d), lambda i, j: (i, 0)), w_spec, *aux_specs],
        out_specs=out_specs, compiler_params=_params(("arbitrary", "arbitrary"), est), name=name)(h, w, *aux)
    return res if tail_rows_per is not None else res[0]


def _swa_prompt_body(sink_ref, q_ref, k_ref, v_ref, o_ref, *, nblk):
    blk = SWA_BLOCK
    row = lax.broadcasted_iota(jnp.int32, (blk, 2 * blk), 0)
    col = lax.broadcasted_iota(jnp.int32, (blk, 2 * blk), 1)
    dist = row + blk - col
    allowed = (dist >= 0) & (dist <= WINDOW)
    distf = dist.astype(f32)
    lane = lax.broadcasted_iota(jnp.int32, (blk, LANES), 1)
    low_half = lane < SWA_HEAD_DIM
    zero = jnp.zeros((blk, LANES), bf16)

    def block(j, carry):
        r0 = pl.multiple_of(j * blk, blk)
        rp = pl.multiple_of(jnp.maximum(j - 1, 0) * blk, blk)
        ok = allowed & ((col >= blk) | (j > 0))
        for g in range(SWA_KV_HEADS):
            gs = slice(g * LANES, (g + 1) * LANES)
            kcat = jnp.concatenate([k_ref[pl.ds(rp, blk), gs], k_ref[pl.ds(r0, blk), gs]], axis=0)
            vcat = jnp.concatenate([v_ref[pl.ds(rp, blk), gs], v_ref[pl.ds(r0, blk), gs]], axis=0)
            for pr in range(SWA_GROUP // 2):
                h0 = g * SWA_GROUP + 2 * pr
                cs = slice(h0 * SWA_HEAD_DIM, h0 * SWA_HEAD_DIM + LANES)
                q2 = q_ref[pl.ds(r0, blk), cs]
                halves = []
                for half in range(2):
                    hh = h0 + half
                    slope = 2.0 ** (-8.0 * (hh + 1) / SWA_HEADS)
                    qm = jnp.where(low_half if half == 0 else ~low_half, q2, zero)
                    s = lax.dot_general(qm, kcat, NT_DIMS, preferred_element_type=f32)
                    s = jnp.where(ok, s - slope * distf, MASKED)
                    sink = sink_ref[hh]
                    mx = jnp.maximum(jnp.max(s, axis=-1, keepdims=True), sink)
                    p = jnp.exp(s - mx)
                    den = jnp.sum(p, axis=-1, keepdims=True) + jnp.exp(sink - mx)
                    o = jnp.dot(p.astype(bf16), vcat, preferred_element_type=f32)
                    halves.append(o / den)
                o_ref[pl.ds(r0, blk), cs] = jnp.where(low_half, halves[0], halves[1]).astype(o_ref.dtype)
        return carry

    lax.fori_loop(0, nblk, block, 0)


def _swa_prompt(q, kd, vd, sinks, nbatch, t):
    nq, nk = q.shape[1], kd.shape[1]
    est = 2 * (2 * _nbytes((t, nq), bf16) + 2 * _nbytes((t, nk), bf16)) + 16 * _nbytes((SWA_BLOCK, 2 * SWA_BLOCK), f32)
    body = functools.partial(_swa_prompt_body, nblk=t // SWA_BLOCK)
    return pl.pallas_call(
        body, out_shape=SDS(q.shape, bf16), grid=(nbatch,),
        in_specs=[pl.BlockSpec(memory_space=pltpu.SMEM), pl.BlockSpec((t, nq), lambda b: (b, 0)),
                  pl.BlockSpec((t, nk), lambda b: (b, 0)), pl.BlockSpec((t, nk), lambda b: (b, 0))],
        out_specs=pl.BlockSpec((t, nq), lambda b: (b, 0)),
        compiler_params=_params(("parallel",), est), name="swa_prompt")(sinks, q, kd, vd)


def _swa_decode_body(q_ref, kn_ref, vn_ref, ck_ref, cv_ref, slope_ref, sink_ref, o_ref, *, bb, w_buf):
    pair_heads = 2 * SWA_GROUP
    lane = lax.broadcasted_iota(jnp.int32, (pair_heads, LANES), 1)
    rowi = lax.broadcasted_iota(jnp.int32, (pair_heads, LANES), 0)
    own_half = (lane < SWA_HEAD_DIM) == (rowi < SWA_GROUP)
    tok = lax.broadcasted_iota(jnp.int32, (pair_heads, w_buf), 1)
    distf = (w_buf - tok).astype(f32)
    for b in range(bb):
        for kp in range(SWA_KV_HEADS // 2):
            ls = slice(kp * LANES, (kp + 1) * LANES)
            hs = slice(kp * pair_heads, (kp + 1) * pair_heads)
            q8 = q_ref[b, hs, :].astype(bf16)
            k2 = ck_ref[b, :, ls].astype(bf16)
            v2 = cv_ref[b, :, ls].astype(bf16)
            kn = kn_ref[b:b + 1, ls].astype(bf16).astype(f32)
            vn = vn_ref[b:b + 1, ls].astype(bf16).astype(f32)
            slope = slope_ref[hs, :]
            sink = sink_ref[hs, 0:1]
            s = lax.dot_general(q8, k2, NT_DIMS, preferred_element_type=f32) - slope[:, :w_buf] * distf
            sn = jnp.sum(q8.astype(f32) * kn, axis=-1, keepdims=True)
            mx = jnp.maximum(jnp.maximum(jnp.max(s, axis=-1, keepdims=True), sn), sink)
            p = jnp.exp(s - mx)
            pn = jnp.exp(sn - mx)
            den = jnp.sum(p, axis=-1, keepdims=True) + pn + jnp.exp(sink - mx)
            o = jnp.dot(p.astype(bf16), v2, preferred_element_type=f32) + pn.astype(bf16).astype(f32) * vn
            o_ref[b, hs, :] = jnp.where(own_half, o / den, 0.0).astype(o_ref.dtype)


def _swa_decode(q_slot, k_new, v_new, cache_k, cache_v, slopes, sinks, bb):
    nb, w_buf, kvd = cache_k.shape
    est = 2 * (2 * _nbytes((bb, w_buf, kvd), f32) + 2 * _nbytes((bb, SWA_HEADS, LANES), f32)) + (1 << 20)
    body = functools.partial(_swa_decode_body, bb=bb, w_buf=w_buf)
    return pl.pallas_call(
        body, out_shape=SDS((nb, SWA_HEADS, LANES), bf16), grid=(nb // bb,),
        in_specs=[pl.BlockSpec((bb, SWA_HEADS, LANES), lambda i: (i, 0, 0)),
                  pl.BlockSpec((bb, kvd), lambda i: (i, 0)), pl.BlockSpec((bb, kvd), lambda i: (i, 0)),
                  pl.BlockSpec((bb, w_buf, kvd), lambda i: (i, 0, 0)), pl.BlockSpec((bb, w_buf, kvd), lambda i: (i, 0, 0)),
                  _resident((SWA_HEADS, LANES)), _resident((SWA_HEADS, LANES))],
        out_specs=pl.BlockSpec((bb, SWA_HEADS, LANES), lambda i: (i, 0, 0)),
        compiler_params=_params(("parallel",), est), name="swa_decode")(q_slot, k_new, v_new, cache_k, cache_v, slopes, sinks)


def _mem_attn_body(q_ref, k_ref, v_ref, o_ref, *, bb):
    for b in range(bb):
        for h in range(MEM_HEADS):
            hs = slice(h * MEM_HEAD_DIM, (h + 1) * MEM_HEAD_DIM)
            q = q_ref[b, :, hs]
            k = k_ref[b, :, hs].astype(bf16)
            v = v_ref[b, :, hs].astype(bf16)
            s = lax.dot_general(q, k, NT_DIMS, preferred_element_type=f32)
            p = jnp.exp(s - jnp.max(s, axis=-1, keepdims=True))
            den = jnp.sum(p, axis=-1, keepdims=True)
            o = jnp.dot(p.astype(bf16), v, preferred_element_type=f32)
            o_ref[b, :, hs] = (o / den).astype(o_ref.dtype)


def _mem_attn(q, k, v, bb, tq):
    nb, t, md = q.shape
    mt = k.shape[1]
    est = 2 * (2 * _nbytes((bb, tq, md), bf16) + 2 * _nbytes((bb, mt, md), f32)) + 4 * _nbytes((tq, mt), f32) + (2 << 20)
    body = functools.partial(_mem_attn_body, bb=bb)
    return pl.pallas_call(
        body, out_shape=SDS((nb, t, md), bf16), grid=(nb // bb, t // tq),
        in_specs=[pl.BlockSpec((bb, tq, md), lambda i, j: (i, j, 0)), pl.BlockSpec((bb, mt, md), lambda i, j: (i, 0, 0)),
                  pl.BlockSpec((bb, mt, md), lambda i, j: (i, 0, 0))],
        out_specs=pl.BlockSpec((bb, tq, md), lambda i, j: (i, j, 0)),
        compiler_params=_params(("parallel", "arbitrary"), est), name="mem_attn")(q, k, v)


def _ssd_gate_norm(y, z, gain):
    y = y * (z * _sigmoid(z))
    gsz = y.shape[1] // SSD_GROUPS
    outs = []
    for g in range(SSD_GROUPS):
        yg = y[:, g * gsz:(g + 1) * gsz]
        ms = jnp.mean(yg * yg, axis=-1, keepdims=True)
        outs.append(yg * lax.rsqrt(ms + EPS) * gain[:, g * gsz:(g + 1) * gsz])
    return jnp.concatenate(outs, axis=-1)


def _ssd_prompt_body(xbc_ref, z_ref, dt_ref, cw_ref, cb_ref, dtb_ref, alog_ref, dsk_ref, gn_ref, rexp_ref, ltri_ref,
                     y_ref, hout_ref, h_scr, xbuf):
    c = pl.program_id(1)
    chunk = SSD_CHUNK
    di = z_ref.shape[1]
    gn_w = SSD_GROUPS * SSD_D_STATE
    gw = di // SSD_GROUPS
    halo = SSD_CONV - 1

    @pl.when(c == 0)
    def _():
        h_scr[...] = jnp.zeros_like(h_scr)
        xbuf[0:SUBLANES, :] = jnp.zeros((SUBLANES, xbuf.shape[1]), f32)

    xr = xbc_ref[...].astype(f32)
    xbuf[SUBLANES:SUBLANES + chunk, :] = xr
    conv = cb_ref[...] + cw_ref[halo:halo + 1, :] * xr
    for k in range(halo):
        conv = conv + cw_ref[k:k + 1, :] * xbuf[SUBLANES - halo + k:SUBLANES - halo + k + chunk, :]
    xbuf[0:SUBLANES, :] = xr[chunk - SUBLANES:, :]
    xc = conv * _sigmoid(conv)
    xs = xc[:, :di]
    bm = xc[:, di:di + gn_w].astype(bf16)
    cm = xc[:, di + gn_w:].astype(bf16)

    dt = _softplus(dt_ref[...] + dtb_ref[...])
    da = dt * (-jnp.exp(alog_ref[...]))
    acs = jnp.dot(ltri_ref[...], da, precision=lax.Precision.HIGHEST, preferred_element_type=f32)
    acs_t = acs.T
    rexp = rexp_ref[...]
    dt_e = _split_dot(dt, rexp)
    eacs_e = _split_dot(jnp.exp(acs), rexp)
    dend_e = _split_dot(jnp.exp(acs[chunk - 1:chunk, :] - acs), rexp)
    xdt = xs * dt_e
    xdt_b = xdt.astype(bf16)
    xw = (xdt * dend_e).astype(bf16)

    row = lax.broadcasted_iota(jnp.int32, (chunk, chunk), 0)
    col = lax.broadcasted_iota(jnp.int32, (chunk, chunk), 1)
    causal = row >= col
    low_half = col < SSD_HEAD_DIM
    heads_per_group = SSD_HEADS // SSD_GROUPS
    ys = []
    for g in range(SSD_GROUPS):
        ns = slice(g * SSD_D_STATE, (g + 1) * SSD_D_STATE)
        ls = slice(g * gw, (g + 1) * gw)
        cb = lax.dot_general(cm[:, ns], bm[:, ns], NT_DIMS, preferred_element_type=f32)
        hg = h_scr[:, ls]
        y_off = jnp.dot(cm[:, ns], hg.astype(bf16), preferred_element_type=f32) * eacs_e[:, ls]
        y_diag = []
        for pr in range(heads_per_group // 2):
            h0 = g * heads_per_group + 2 * pr
            xp = xdt_b[:, h0 * SSD_HEAD_DIM:h0 * SSD_HEAD_DIM + LANES]
            halves = []
            for hh in (h0, h0 + 1):
                diff = acs[:, hh:hh + 1] - acs_t[hh:hh + 1, :]
                w = (cb * jnp.exp(jnp.where(causal, diff, MASKED))).astype(bf16)
                halves.append(jnp.dot(w, xp, preferred_element_type=f32))
            y_diag.append(jnp.where(low_half, halves[0], halves[1]))
        st = lax.dot_general(bm[:, ns], xw[:, ls], TN_DIMS, preferred_element_type=f32)
        h_scr[:, ls] = eacs_e[chunk - 1:chunk, ls] * hg + st
        ys.append(jnp.concatenate(y_diag, axis=-1) + y_off)
    y = jnp.concatenate(ys, axis=-1) + dsk_ref[...] * xs
    y_ref[...] = _ssd_gate_norm(y, z_ref[...].astype(f32), gn_ref[...]).astype(y_ref.dtype)

    @pl.when(c == pl.num_programs(1) - 1)
    def _():
        hout_ref[0] = h_scr[...].T


def _ssd_prompt(xbc, z, dt, ssd_w, nbatch, t):
    cw, cb, dtb, alog, dsk, gn, rexp, ltri = ssd_w
    ch, di = xbc.shape[1], z.shape[1]
    nchunk = t // SSD_CHUNK
    rows = lambda b, c: (b * nchunk + c, 0)
    est = (2 * (_nbytes((SSD_CHUNK, ch), bf16) + 2 * _nbytes((SSD_CHUNK, di), bf16) + _nbytes((di, SSD_D_STATE), f32))
           + _nbytes((SSD_D_STATE, di), f32) + 24 * _nbytes((SSD_CHUNK, ch), f32))
    return pl.pallas_call(
        _ssd_prompt_body,
        out_shape=[SDS((nbatch * t, di), bf16), SDS((nbatch, di, SSD_D_STATE), f32)],
        grid=(nbatch, nchunk),
        in_specs=[pl.BlockSpec((SSD_CHUNK, ch), rows), pl.BlockSpec((SSD_CHUNK, di), rows),
                  pl.BlockSpec((SSD_CHUNK, LANES), rows),
                  _resident(cw.shape), _resident(cb.shape), _resident(dtb.shape), _resident(alog.shape),
                  _resident(dsk.shape), _resident(gn.shape), _resident(rexp.shape), _resident(ltri.shape)],
        out_specs=[pl.BlockSpec((SSD_CHUNK, di), rows), pl.BlockSpec((1, di, SSD_D_STATE), lambda b, c: (b, 0, 0))],
        scratch_shapes=[pltpu.VMEM((SSD_D_STATE, di), f32), pltpu.VMEM((SUBLANES + SSD_CHUNK, ch), f32)],
        compiler_params=_params(("parallel", "arbitrary"), est), name="ssd_prompt")(xbc, z, dt, cw, cb, dtb, alog, dsk, gn, rexp, ltri)


def _ssd_step_body(xbc_ref, z_ref, dt_ref, cs_ref, h_ref, cw_ref, cb_ref, dtb_ref, alog_ref, dsk_ref, gn_ref, rexp_ref,
                   y_ref, cso_ref, ho_ref, xdt_scr, da_scr, bm_scr, cm_scr, y_scr, *, bb):
    ch = xbc_ref.shape[1]
    di = z_ref.shape[1]
    gn_w = SSD_GROUPS * SSD_D_STATE
    gw = di // SSD_GROUPS
    halo = SSD_CONV - 1
    xr = xbc_ref[...]
    conv = cb_ref[...] + cw_ref[halo:halo + 1, :] * xr
    for k in range(halo):
        conv = conv + cw_ref[k:k + 1, :] * cs_ref[:, k * ch:(k + 1) * ch]
    for k in range(1, halo):
        cso_ref[:, (k - 1) * ch:k * ch] = cs_ref[:, k * ch:(k + 1) * ch]
    cso_ref[:, (halo - 1) * ch:] = xr
    xc = conv * _sigmoid(conv)
    xs = xc[:, :di]
    dt = _softplus(dt_ref[...] + dtb_ref[...])
    rexp = rexp_ref[...]
    xdt_scr[...] = xs * _split_dot(dt, rexp)
    da_scr[...] = _split_dot(jnp.exp(dt * (-jnp.exp(alog_ref[...]))), rexp)
    bm_scr[...] = xc[:, di:di + gn_w]
    cm_scr[...] = xc[:, di + gn_w:]

    rows = 2 * SUBLANES
    rowi = lax.broadcasted_iota(jnp.int32, (rows, di), 0)
    grp = lax.broadcasted_iota(jnp.int32, (rows, di), 1) // gw
    rown = lax.broadcasted_iota(jnp.int32, (rows, SSD_D_STATE), 0)
    ones_rows = jnp.where((rown == SSD_GROUPS) | (rown == SSD_GROUPS + 1), 1.0, 0.0).astype(f32)

    def per_row(b, carry):
        xrow = xdt_scr[pl.ds(b, 1), :]
        drow = da_scr[pl.ds(b, 1), :]
        d_hi = drow.astype(bf16).astype(f32)
        lhs = jnp.where(rowi == grp, xrow, jnp.where(rowi == SSD_GROUPS, d_hi, jnp.where(rowi == SSD_GROUPS + 1, drow - d_hi, 0.0)))
        brow = bm_scr[pl.ds(b, 1), :]
        crow = cm_scr[pl.ds(b, 1), :]
        rhs_b = jnp.zeros((rows, SSD_D_STATE), f32)
        c_rows = jnp.zeros((rows, SSD_D_STATE), f32)
        for g in range(SSD_GROUPS):
            ns = slice(g * SSD_D_STATE, (g + 1) * SSD_D_STATE)
            rhs_b = jnp.where(rown == g, brow[:, ns], rhs_b)
            c_rows = jnp.where(rown == g, crow[:, ns], c_rows)
        rhs = jnp.concatenate([rhs_b, ones_rows], axis=1).astype(bf16)
        sd = lax.dot_general(lhs.astype(bf16), rhs, TN_DIMS, preferred_element_type=f32)
        hn = sd[:, SSD_D_STATE:] * h_ref[b] + sd[:, :SSD_D_STATE]
        ho_ref[b] = hn
        y8 = lax.dot_general(c_rows.astype(bf16), hn.astype(bf16), NT_DIMS, preferred_element_type=f32)
        y_scr[pl.ds(b, 1), :] = jnp.sum(jnp.where(rowi == grp, y8, 0.0), axis=0, keepdims=True)
        return carry

    lax.fori_loop(0, bb, per_row, 0)
    y = y_scr[...] + dsk_ref[...] * xs
    y_ref[...] = _ssd_gate_norm(y, z_ref[...], gn_ref[...]).astype(y_ref.dtype)


def _ssd_step(xbc, z, dt, conv_state, h0, ssd_w, bb):
    cw, cb, dtb, alog, dsk, gn, rexp, _ = ssd_w
    nb, ch = xbc.shape
    di = z.shape[1]
    gn_w = SSD_GROUPS * SSD_D_STATE
    halo = SSD_CONV - 1
    est = (2 * (2 * _nbytes((bb, di, SSD_D_STATE), f32) + 2 * _nbytes((bb, halo * ch), f32) + 4 * _nbytes((bb, ch), f32))
           + 8 * _nbytes((di, 2 * SSD_D_STATE), f32))
    body = functools.partial(_ssd_step_body, bb=bb)
    r2 = lambda i: (i, 0)
    return pl.pallas_call(
        body,
        out_shape=[SDS((nb, di), bf16), SDS((nb, halo * ch), f32), SDS((nb, di, SSD_D_STATE), f32)],
        grid=(nb // bb,),
        in_specs=[pl.BlockSpec((bb, ch), r2), pl.BlockSpec((bb, di), r2), pl.BlockSpec((bb, LANES), r2),
                  pl.BlockSpec((bb, halo * ch), r2), pl.BlockSpec((bb, di, SSD_D_STATE), lambda i: (i, 0, 0)),
                  _resident(cw.shape), _resident(cb.shape), _resident(dtb.shape), _resident(alog.shape),
                  _resident(dsk.shape), _resident(gn.shape), _resident(rexp.shape)],
        out_specs=[pl.BlockSpec((bb, di), r2), pl.BlockSpec((bb, halo * ch), r2),
                   pl.BlockSpec((bb, di, SSD_D_STATE), lambda i: (i, 0, 0))],
        scratch_shapes=[pltpu.VMEM((bb, di), f32), pltpu.VMEM((bb, di), f32), pltpu.VMEM((bb, gn_w), f32),
                        pltpu.VMEM((bb, gn_w), f32), pltpu.VMEM((bb, di), f32)],
        compiler_params=_params(("parallel",), est), name="ssd_step")(xbc, z, dt, conv_state, h0, cw, cb, dtb, alog, dsk, gn, rexp)


def _merge_body(a_ref, s_ref, m_ref, g0_ref, g1_ref, g2_ref, wa_ref, ws_ref, wm_ref, o_ref, *, chunk):
    a, s, m = a_ref[...], s_ref[...], m_ref[...]
    for c0 in range(0, o_ref.shape[1], chunk):
        cs = slice(c0, c0 + chunk)
        acc = g0_ref[:, cs].astype(f32) * jnp.dot(a, wa_ref[:, cs], preferred_element_type=f32)
        acc = acc + g1_ref[:, cs].astype(f32) * jnp.dot(s, ws_ref[:, cs], preferred_element_type=f32)
        acc = acc + g2_ref[:, cs].astype(f32) * jnp.dot(m, wm_ref[:, cs], preferred_element_type=f32)
        o_ref[:, cs] = acc.astype(o_ref.dtype)


def _merge(a, s, mo, gates, wa, ws, wm, tm):
    m, d = s.shape[0], wa.shape[1]
    chunk = 512
    est = (2 * (_nbytes((tm, a.shape[1]), bf16) + _nbytes((tm, s.shape[1]), bf16) + _nbytes((tm, mo.shape[1]), bf16)
                + 4 * _nbytes((tm, d), bf16))
           + _nbytes(wa.shape, bf16) + _nbytes(ws.shape, bf16) + _nbytes(wm.shape, bf16) + 6 * _nbytes((tm, chunk), f32))
    body = functools.partial(_merge_body, chunk=chunk)
    return pl.pallas_call(
        body, out_shape=SDS((m, d), bf16), grid=(m // tm,),
        in_specs=[pl.BlockSpec((tm, a.shape[1]), lambda i: (i, 0)), pl.BlockSpec((tm, s.shape[1]), lambda i: (i, 0)),
                  pl.BlockSpec((tm, mo.shape[1]), lambda i: (i, 0)),
                  pl.BlockSpec((tm, d), lambda i: (i, 0)), pl.BlockSpec((tm, d), lambda i: (i, 1)),
                  pl.BlockSpec((tm, d), lambda i: (i, 2)),
                  _resident(wa.shape), _resident(ws.shape), _resident(wm.shape)],
        out_specs=pl.BlockSpec((tm, d), lambda i: (i, 0)),
        compiler_params=_params(("parallel",), est), name="merge")(a, s, mo, gates, gates, gates, wa, ws, wm)


def _ffn_body(x_ref, g_ref, wg_ref, wu_ref, wd_ref, o_ref, h_scr):
    @pl.when(pl.program_id(1) == 0)
    def _():
        x = x_ref[...]
        ms = jnp.mean(x * x, axis=-1, keepdims=True)
        h_scr[...] = (x * lax.rsqrt(ms + EPS) * g_ref[...]).astype(bf16)
        o_ref[...] = x

    h = h_scr[...]
    gate = jnp.dot(h, wg_ref[...], preferred_element_type=f32)
    up = jnp.dot(h, wu_ref[...], preferred_element_type=f32)
    act = (gate * _sigmoid(gate) * up).astype(bf16)
    o_ref[...] += jnp.dot(act, wd_ref[...], preferred_element_type=f32)


def _ffn(x, gain, wg, wu, wd, tm, tf):
    m, d = x.shape
    dff = wg.shape[1]
    est = (4 * _nbytes((tm, d), f32) + _nbytes((tm, d), bf16) + 2 * 3 * _nbytes((d, tf), bf16)
           + 3 * _nbytes((tm, tf), f32) + _nbytes((tm, d), f32))
    return pl.pallas_call(
        _ffn_body, out_shape=SDS((m, d), f32), grid=(m // tm, dff // tf),
        in_specs=[pl.BlockSpec((tm, d), lambda i, c: (i, 0)), _resident((1, d)),
                  pl.BlockSpec((d, tf), lambda i, c: (0, c)), pl.BlockSpec((d, tf), lambda i, c: (0, c)),
                  pl.BlockSpec((tf, d), lambda i, c: (c, 0))],
        out_specs=pl.BlockSpec((tm, d), lambda i, c: (i, 0)),
        scratch_shapes=[pltpu.VMEM((tm, d), bf16)],
        compiler_params=_params(("parallel", "arbitrary"), est), name="ffn")(x, gain.reshape(1, d), wg, wu, wd)


def _dup_heads(w):
    lead = w.shape[:-1]
    w = w.reshape(*lead, SWA_KV_HEADS, 1, SWA_HEAD_DIM)
    return jnp.broadcast_to(w, (*lead, SWA_KV_HEADS, 2, SWA_HEAD_DIM)).reshape(*lead, SWA_KV_HEADS * LANES)


def _slot_heads(w, axis):
    w = jnp.moveaxis(w, axis, -1)
    lead = w.shape[:-1]
    w = w.reshape(*lead, SWA_HEADS, 1, SWA_HEAD_DIM)
    kv_half = (jnp.arange(SWA_HEADS) // SWA_GROUP) % 2
    sel = (kv_half[:, None] == jnp.arange(2)[None, :]).astype(w.dtype)[:, :, None]
    w = (w * sel).reshape(*lead, SWA_HEADS * LANES)
    return jnp.moveaxis(w, -1, axis)


def _pad_lanes(v, n=LANES):
    return jnp.pad(v, [(0, 0)] * (v.ndim - 1) + [(0, n - v.shape[-1])])


def _layer(xp, xs, cache_k, cache_v, cmem_k, cmem_v, state_ssm, state_conv, mem_prompt, w):
    bp, tp, d = xp.shape
    bs = xs.shape[0]
    w_buf = cache_k.shape[1]
    assert xs.shape[1] == 1 and w_buf == WINDOW and tp % SSD_CHUNK == 0 and bs % SUBLANES == 0
    q_dim = SWA_HEADS * SWA_HEAD_DIM
    kv_dim = SWA_KV_HEADS * SWA_HEAD_DIM
    di = SSD_HEADS * SSD_HEAD_DIM
    ch = di + 2 * SSD_GROUPS * SSD_D_STATE
    md = MEM_HEADS * MEM_HEAD_DIM
    o = 0
    w_in = w["w_in"]
    wq, o = w_in[:, o:o + q_dim], o + q_dim
    wk, o = w_in[:, o:o + kv_dim], o + kv_dim
    wv, o = w_in[:, o:o + kv_dim], o + kv_dim
    wz, o = w_in[:, o:o + di], o + di
    wxbc, o = w_in[:, o:o + ch], o + ch
    wdt, o = w_in[:, o:o + SSD_HEADS], o + SSD_HEADS
    wqm, o = w_in[:, o:o + md], o + md
    wg = w_in[:, o:]
    wdt = _pad_lanes(wdt)

    q_scale = SWA_HEAD_DIM ** -0.5
    gq = jnp.tile(w["q_norm_swa"], SWA_HEADS) * q_scale
    gk = jnp.tile(w["k_norm_swa"], SWA_KV_HEADS)
    half = jnp.arange(LANES) // SWA_HEAD_DIM
    blockdiag = (half[:, None] == half[None, :]).astype(bf16)
    w_qkv_p = jnp.concatenate([wq, _dup_heads(wk), _dup_heads(wv), wdt], axis=1).astype(bf16)
    w_qkv_s = jnp.concatenate([_slot_heads(wq, 1), wk, wv, wdt], axis=1).astype(bf16)
    gqm = (jnp.tile(w["q_norm_mem"], MEM_HEADS) * MEM_HEAD_DIM ** -0.5).reshape(1, md)
    gkm = jnp.tile(w["k_norm_mem"], MEM_HEADS).reshape(1, md)
    wz_b, wxbc_b, wqm_b, wg_b = wz.astype(bf16), wxbc.astype(bf16), wqm.astype(bf16), wg.astype(bf16)
    w_mem_k, w_mem_v = w["w_mem_kv"][:, :md].astype(bf16), w["w_mem_kv"][:, md:].astype(bf16)
    wa, ws, wm = w["w_up_swa"].astype(bf16), w["w_up_ssd"].astype(bf16), w["w_up_mem"].astype(bf16)
    wa_slot = _slot_heads(w["w_up_swa"], 0).astype(bf16)
    w_out = w["w_out"].astype(bf16)
    w_gate, w_up, w_down = w["w_gate"].astype(bf16), w["w_up"].astype(bf16), w["w_down"].astype(bf16)

    head_of_lane = jnp.arange(di) // SSD_HEAD_DIM
    rexp = (jnp.arange(LANES)[:, None] == head_of_lane[None, :]).astype(bf16)
    ltri = (jnp.arange(SSD_CHUNK)[:, None] >= jnp.arange(SSD_CHUNK)[None, :]).astype(f32)
    ssd_w = (w["conv_w"], w["conv_b"].reshape(1, ch), _pad_lanes(w["dt_bias"].reshape(1, -1)),
             _pad_lanes(w["a_log"].reshape(1, -1)), jnp.repeat(w["d_skip"], SSD_HEAD_DIM).reshape(1, di),
             w["ssd_norm"].reshape(1, di), rexp, ltri)
    slopes = jnp.broadcast_to(jnp.exp2(-8.0 * jnp.arange(1, SWA_HEADS + 1, dtype=f32) / SWA_HEADS)[:, None], (SWA_HEADS, LANES))
    sinks_b = jnp.broadcast_to(w["swa_sinks"].astype(f32)[:, None], (SWA_HEADS, LANES))

    def finish(x2, a_out, s_out, m_out, gates, wa_, tm, tm_ffn):
        merged = _merge(a_out, s_out, m_out, gates, wa_, ws, wm, tm)
        x1 = _proj(merged, w_out, _epi_residual, f32, tm, d, aux=(x2,), aux_specs=(pl.BlockSpec((tm, d), lambda i, j: (i, 0)),),
                   name="out_proj")
        return _ffn(x1, w["norm_ffn"], w_gate, w_up, w_down, tm_ffn, 512)

    mp = bp * tp
    x2 = xp.reshape(mp, d)
    tm = min(1024, tp)
    tpb = tp // tm
    h = _rmsnorm(x2, w["norm_mix"], min(512, tp))
    q, kd, vd, dt, klast, vlast = _qkv_proj(h, w_qkv_p, gq.reshape(1, -1), _dup_heads(gk).reshape(1, -1), blockdiag,
                                            q_dim, 2 * kv_dim, 2 * kv_dim, tm, tpb, bf16, bf16, True)
    z = _proj(h, wz_b, _epi_plain, bf16, tm, di, name="z_proj")
    xbc, xtail = _proj(h, wxbc_b, _epi_plain_tail, bf16, tm, ch, tail_rows_per=tpb, name="xbc_proj")
    qm = _proj(h, wqm_b, _epi_headnorm256, bf16, tm, md, aux=(gqm,), aux_specs=(_resident((1, md)),), name="qm_proj")
    gates = _proj(h, wg_b, _epi_sigmoid, bf16, tm, d, name="gate_proj")
    a_out = _swa_prompt(q, kd, vd, w["swa_sinks"].astype(f32), bp, tp)
    s_out, p_h = _ssd_prompt(xbc, z, dt, ssd_w, bp, tp)
    mt = mem_prompt.shape[1]
    hm = _rmsnorm(mem_prompt.reshape(bp * mt, d), w["norm_mem"], mt)
    mk = _proj(hm, w_mem_k, _epi_headnorm256, f32, mt, md, aux=(gkm,), aux_specs=(_resident((1, md)),), name="mem_k_proj")
    mv = _proj(hm, w_mem_v, _epi_plain, f32, mt, md, name="mem_v_proj")
    m_out = _mem_attn(qm.reshape(bp, tp, md), mk.reshape(bp, mt, md), mv.reshape(bp, mt, md), 1, tm)
    yp = finish(x2, a_out, s_out, m_out.reshape(mp, md), gates, wa, min(512, tp), min(512, tp)).reshape(bp, tp, d)

    undup = lambda t: t.reshape(bp, WINDOW, SWA_KV_HEADS, 2, SWA_HEAD_DIM)[:, :, :, 0, :]
    p_k, p_v = undup(klast), undup(vlast)
    p_mk = mk.reshape(bp, mt, MEM_HEADS, MEM_HEAD_DIM)
    p_mv = mv.reshape(bp, mt, MEM_HEADS, MEM_HEAD_DIM)
    p_h = p_h.reshape(bp, SSD_HEADS, SSD_HEAD_DIM, SSD_D_STATE)
    p_c = xtail.reshape(bp, SUBLANES, ch)[:, SUBLANES - (SSD_CONV - 1):, :]

    xs2 = xs.reshape(bs, d)
    hs = _rmsnorm(xs2, w["norm_mix"], bs)
    qs, ks, vs, dts = _qkv_proj(hs, w_qkv_s, _slot_heads(gq, 0).reshape(1, -1), gk.reshape(1, -1), blockdiag,
                                SWA_HEADS * LANES, kv_dim, kv_dim, bs, 1, f32, f32, False)
    zs = _proj(hs, wz_b, _epi_plain, f32, bs, di, name="z_proj_s")
    xbcs = _proj(hs, wxbc_b, _epi_plain, f32, bs, ch, name="xbc_proj_s")
    qms = _proj(hs, wqm_b, _epi_headnorm256, bf16, bs, md, aux=(gqm,), aux_specs=(_resident((1, md)),), name="qm_proj_s")
    gates_s = _proj(hs, wg_b, _epi_sigmoid, bf16, bs, d, name="gate_proj_s")
    ck = cache_k.reshape(bs, w_buf, kv_dim)
    cv = cache_v.reshape(bs, w_buf, kv_dim)
    a_s = _swa_decode(qs.reshape(bs, SWA_HEADS, LANES), ks, vs, ck, cv, slopes, sinks_b, SUBLANES)
    halo = SSD_CONV - 1
    s_s, s_c, s_h = _ssd_step(xbcs, zs, dts, state_conv.reshape(bs, halo * ch), state_ssm.reshape(bs, di, SSD_D_STATE),
                              ssd_w, SUBLANES)
    qm8 = jnp.broadcast_to(qms[:, None, :], (bs, 2 * SUBLANES, md))
    m_s = _mem_attn(qm8, cmem_k.reshape(bs, -1, md), cmem_v.reshape(bs, -1, md), 4, 2 * SUBLANES)[:, 0, :]
    ys = finish(xs2, a_s.reshape(bs, SWA_HEADS * LANES), s_s, m_s, gates_s, wa_slot, bs, bs).reshape(bs, 1, d)

    s_k = jnp.concatenate([ck[:, 1:], ks[:, None, :]], axis=1).reshape(bs, w_buf, SWA_KV_HEADS, SWA_HEAD_DIM)
    s_v = jnp.concatenate([cv[:, 1:], vs[:, None, :]], axis=1).reshape(bs, w_buf, SWA_KV_HEADS, SWA_HEAD_DIM)
    s_h = s_h.reshape(bs, SSD_HEADS, SSD_HEAD_DIM, SSD_D_STATE)
    s_c = s_c.reshape(bs, halo, ch)
    return yp, ys, (p_k, p_v, p_mk, p_mv, p_h, p_c), (s_k, s_v, s_h, s_c)


_WEIGHT_NAMES = ("norm_mix", "w_in", "q_norm_swa", "k_norm_swa", "swa_sinks", "conv_w", "conv_b", "dt_bias", "a_log",
                 "d_skip", "ssd_norm", "norm_mem", "w_mem_kv", "q_norm_mem", "k_norm_mem", "w_up_swa", "w_up_ssd",
                 "w_up_mem", "w_out", "norm_ffn", "w_gate", "w_up", "w_down")


def kernel(x_prompt, x_sample, cache_swa_k, cache_swa_v, cache_mem_k, cache_mem_v, state_ssm, state_conv, mem_prompt, norm_mix, w_in, q_norm_swa, k_norm_swa, swa_sinks, conv_w, conv_b, dt_bias, a_log, d_skip, ssd_norm, norm_mem, w_mem_kv, q_norm_mem, k_norm_mem, w_up_swa, w_up_ssd, w_up_mem, w_out, norm_ffn, w_gate, w_up, w_down):
    weights = (norm_mix, w_in, q_norm_swa, k_norm_swa, swa_sinks, conv_w, conv_b, dt_bias, a_log, d_skip, ssd_norm,
               norm_mem, w_mem_kv, q_norm_mem, k_norm_mem, w_up_swa, w_up_ssd, w_up_mem, w_out, norm_ffn, w_gate, w_up, w_down)
    depth = w_in.shape[0]
    yp, ys = x_prompt, x_sample
    p_outs, s_outs = [], []
    for l in range(depth):
        w = {n: a[l] for n, a in zip(_WEIGHT_NAMES, weights)}
        yp, ys, po, so = _layer(yp, ys, cache_swa_k[l], cache_swa_v[l], cache_mem_k[l], cache_mem_v[l],
                                state_ssm[l], state_conv[l], mem_prompt, w)
        p_outs.append(po)
        s_outs.append(so)
    stack = lambda outs, i: jnp.stack([o[i] for o in outs])
    return (yp, ys, *(stack(p_outs, i) for i in range(6)), *(stack(s_outs, i) for i in range(4)))
```

```python
import functools

import jax
import jax.numpy as jnp
from jax import lax
from jax.experimental import pallas as pl
from jax.experimental.pallas import tpu as pltpu

f32 = jnp.float32
bf16 = jnp.bfloat16
SDS = jax.ShapeDtypeStruct

LANES = 128
SUBLANES = 8
VMEM_BYTES_V7X = 64 * 1024 * 1024
VMEM_HEADROOM = 8 * 1024 * 1024

EPS = 1e-6
SWA_HEADS = 16
SWA_KV_HEADS = 4
SWA_GROUP = SWA_HEADS // SWA_KV_HEADS
SWA_HEAD_DIM = 64
WINDOW = 128
SWA_BLOCK = 128
SSD_HEAD_DIM = 64
SSD_HEADS = 32
SSD_GROUPS = 4
SSD_D_STATE = 128
SSD_CONV = 4
SSD_CHUNK = 128
MEM_HEADS = 4
MEM_HEAD_DIM = 256
N_BRANCH = 3
MASKED = -1e30
LOG2_E = 1.4426950408889634

NT_DIMS = (((1,), (1,)), ((), ()))
TN_DIMS = (((0,), (0,)), ((), ()))


def _params(semantics, block_bytes):
    limit = min(int(block_bytes) + VMEM_HEADROOM, VMEM_BYTES_V7X - VMEM_HEADROOM)
    return pltpu.CompilerParams(dimension_semantics=semantics, vmem_limit_bytes=limit)


def _nbytes(shape, dtype):
    n = 1
    for s in shape:
        n *= s
    return n * jnp.dtype(dtype).itemsize


def _resident(shape):
    nd = len(shape)
    return pl.BlockSpec(shape, lambda *_: (0,) * nd, pipeline_mode=pl.Buffered(1))


def _sigmoid(x):
    return 0.5 + 0.5 * jnp.tanh(0.5 * x)


def _silu(x):
    hx = 0.5 * x
    return hx + hx * jnp.tanh(hx)


def _softplus(x):
    return jnp.maximum(x, 0.0) + jnp.log1p(jnp.exp(-jnp.abs(x)))


def _split_dot(v, m):
    hi = v.astype(bf16)
    lo = (v - hi.astype(f32)).astype(bf16)
    return jnp.dot(hi, m, preferred_element_type=f32) + jnp.dot(lo, m, preferred_element_type=f32)


def _headnorm64(acc, gain, blockdiag):
    outs = []
    wide = blockdiag.shape[0]
    for c in range(acc.shape[1] // wide):
        a = acc[:, c * wide:(c + 1) * wide]
        ss = jnp.dot((a * a).astype(bf16), blockdiag, preferred_element_type=f32)
        outs.append(a * lax.rsqrt(ss * (1.0 / SWA_HEAD_DIM) + EPS) * gain[:, c * wide:(c + 1) * wide])
    return outs[0] if len(outs) == 1 else jnp.concatenate(outs, axis=-1)


def _norm_body(x_ref, g_ref, o_ref):
    x = x_ref[...]
    ms = jnp.mean(x * x, axis=-1, keepdims=True)
    o_ref[...] = (x * lax.rsqrt(ms + EPS) * g_ref[...]).astype(o_ref.dtype)


def _rmsnorm(x, gain, tm):
    m, d = x.shape
    est = 2 * (_nbytes((tm, d), f32) + _nbytes((tm, d), bf16)) + _nbytes((tm, d), f32)
    return pl.pallas_call(
        _norm_body, out_shape=SDS((m, d), bf16), grid=(m // tm,),
        in_specs=[pl.BlockSpec((tm, d), lambda i: (i, 0)), _resident((1, d))],
        out_specs=pl.BlockSpec((tm, d), lambda i: (i, 0)),
        compiler_params=_params(("parallel",), est), name="rmsnorm")(x, gain.reshape(1, d))


def _qkv_body(h_ref, w_ref, gq_ref, gk_ref, bd_ref, q_ref, k_ref, *rest, nq, nk, nv, chunk, prompt):
    v_ref, dt_ref, *last_refs = rest
    h = h_ref[...]
    bd = bd_ref[...]
    tm = h.shape[0]
    for c0 in range(0, nq, chunk):
        acc = jnp.dot(h, w_ref[:, c0:c0 + chunk], preferred_element_type=f32)
        q_ref[:, c0:c0 + chunk] = _headnorm64(acc, gq_ref[:, c0:c0 + chunk], bd).astype(q_ref.dtype)
    for c0 in range(0, nk, chunk):
        w = min(chunk, nk - c0)
        acc = jnp.dot(h, w_ref[:, nq + c0:nq + c0 + w], preferred_element_type=f32)
        kn = _headnorm64(acc, gk_ref[:, c0:c0 + w], bd)
        k_ref[:, c0:c0 + w] = kn.astype(k_ref.dtype)
        if last_refs:
            last_refs[0][:, c0:c0 + w] = kn[tm - WINDOW:, :]
    for c0 in range(0, nv, chunk):
        w = min(chunk, nv - c0)
        acc = jnp.dot(h, w_ref[:, nq + nk + c0:nq + nk + c0 + w], preferred_element_type=f32)
        if prompt:
            low_half = lax.broadcasted_iota(jnp.int32, (tm, LANES), 1) < SWA_HEAD_DIM
            for t0 in range(0, w, LANES):
                a = acc[:, t0:t0 + LANES]
                v_ref[:, 2 * (c0 + t0):2 * (c0 + t0) + LANES] = jnp.where(low_half, a, 1.0).astype(v_ref.dtype)
                v_ref[:, 2 * (c0 + t0) + LANES:2 * (c0 + t0 + LANES)] = jnp.where(low_half, 1.0, a).astype(v_ref.dtype)
            last_refs[1][:, c0:c0 + w] = acc[tm - WINDOW:, :]
        else:
            v_ref[:, c0:c0 + w] = acc.astype(v_ref.dtype)
    dt_ref[...] = jnp.dot(h, w_ref[:, nq + nk + nv:], preferred_element_type=f32)


def _qkv_proj(h, w, gq, gk, blockdiag, nq, nk, nv, tm, tiles_per_batch, q_dtype, kv_dtype, prompt):
    m, d = h.shape
    n = w.shape[1]
    chunk = 512
    rows = lambda width: pl.BlockSpec((tm, width), lambda i: (i, 0))
    v_width = 2 * nv if prompt else nv
    out_shape = [SDS((m, nq), q_dtype), SDS((m, nk), kv_dtype), SDS((m, v_width), kv_dtype), SDS((m, LANES), f32)]
    out_specs = [rows(nq), rows(nk), rows(v_width), rows(LANES)]
    if prompt:
        nb = m // (tm * tiles_per_batch)
        out_shape += [SDS((nb * WINDOW, nk), f32), SDS((nb * WINDOW, nv), f32)]
        out_specs += [pl.BlockSpec((WINDOW, nk), lambda i: (i // tiles_per_batch, 0)),
                      pl.BlockSpec((WINDOW, nv), lambda i: (i // tiles_per_batch, 0))]
    est = (2 * _nbytes((tm, d), bf16) + _nbytes((d, n), bf16) + 2 * _nbytes((tm, nq + nk + 2 * nv), f32)
           + 4 * _nbytes((tm, chunk), f32))
    body = functools.partial(_qkv_body, nq=nq, nk=nk, nv=nv, chunk=chunk, prompt=prompt)
    return pl.pallas_call(
        body, out_shape=out_shape, grid=(m // tm,),
        in_specs=[pl.BlockSpec((tm, d), lambda i: (i, 0)), _resident((d, n)), _resident((1, nq)),
                  _resident((1, nk)), _resident(blockdiag.shape)],
        out_specs=out_specs, compiler_params=_params(("arbitrary",), est), name="qkv_proj")(h, w, gq, gk, blockdiag)


def _proj_body(h_ref, w_ref, *refs, epilogue, chunk, n_aux):
    aux, outs = refs[:n_aux], refs[n_aux:]
    h = h_ref[...]
    tn = w_ref.shape[1]
    for c0 in range(0, tn, chunk):
        acc = jnp.dot(h, w_ref[:, c0:c0 + chunk], preferred_element_type=f32)
        epilogue(acc, c0, chunk, aux, outs)


def _epi_plain(acc, c0, w, aux, outs):
    outs[0][:, c0:c0 + w] = acc.astype(outs[0].dtype)


def _epi_silu(acc, c0, w, aux, outs):
    outs[0][:, c0:c0 + w] = _silu(acc).astype(outs[0].dtype)


def _epi_sigmoid(acc, c0, w, aux, outs):
    outs[0][:, c0:c0 + w] = _sigmoid(acc).astype(outs[0].dtype)


def _epi_headnorm256(acc, c0, w, aux, outs):
    gain = aux[0]
    for h0 in range(0, w, MEM_HEAD_DIM):
        a = acc[:, h0:h0 + MEM_HEAD_DIM]
        ms = jnp.mean(a * a, axis=-1, keepdims=True)
        y = a * lax.rsqrt(ms + EPS) * gain[:, c0 + h0:c0 + h0 + MEM_HEAD_DIM]
        outs[0][:, c0 + h0:c0 + h0 + MEM_HEAD_DIM] = y.astype(outs[0].dtype)


def _epi_residual(acc, c0, w, aux, outs):
    outs[0][:, c0:c0 + w] = aux[0][:, c0:c0 + w] + acc


def _proj(h, w, epilogue, out_dtype, tm, tn, aux=(), aux_specs=(), chunk=512, name="proj"):
    m, d = h.shape
    n = w.shape[1]
    nj = n // tn
    chunk = min(chunk, tn)
    w_spec = _resident((d, n)) if nj == 1 else pl.BlockSpec((d, tn), lambda i, j: (0, j))
    est = (2 * _nbytes((tm, d), bf16) + (1 if nj == 1 else 2) * _nbytes((d, tn), bf16)
           + 2 * _nbytes((tm, tn), out_dtype) + 4 * _nbytes((tm, chunk), f32)
           + sum(2 * _nbytes(s.block_shape, f32) for s in aux_specs))
    body = functools.partial(_proj_body, epilogue=epilogue, chunk=chunk, n_aux=len(aux))
    return pl.pallas_call(
        body, out_shape=SDS((m, n), out_dtype), grid=(m // tm, nj),
        in_specs=[pl.BlockSpec((tm, d), lambda i, j: (i, 0)), w_spec, *aux_specs],
        out_specs=pl.BlockSpec((tm, tn), lambda i, j: (i, j)),
        compiler_params=_params(("parallel", "arbitrary"), est), name=name)(h, w, *aux)


def _xbc_body(h_ref, w_ref, cw_ref, cb_ref, o_ref, tail_ref, halo_scr, xbuf, *, chunk, tiles_per_batch):
    tm = h_ref.shape[0]
    halo = SSD_CONV - 1

    @pl.when(lax.rem(pl.program_id(0), tiles_per_batch) == 0)
    def _():
        halo_scr[...] = jnp.zeros_like(halo_scr)

    h = h_ref[...]
    for c0 in range(0, w_ref.shape[1], chunk):
        cs = slice(c0, c0 + chunk)
        acc = jnp.dot(h, w_ref[:, cs], preferred_element_type=f32)
        last = acc[tm - SUBLANES:, :]
        tail_ref[:, cs] = last
        xbuf[0:SUBLANES, :] = halo_scr[:, cs]
        xbuf[SUBLANES:, :] = acc
        halo_scr[:, cs] = last
        conv = cb_ref[:, cs] + cw_ref[halo:halo + 1, cs] * acc
        for k in range(halo):
            conv = conv + cw_ref[k:k + 1, cs] * xbuf[SUBLANES - halo + k:SUBLANES - halo + k + tm, :]
        o_ref[:, cs] = _silu(conv).astype(o_ref.dtype)


def _xbc_proj(h, w, conv_w, conv_b, tm, tiles_per_batch):
    m, d = h.shape
    n = w.shape[1]
    chunk = 512
    nb = m // (tm * tiles_per_batch)
    est = (2 * _nbytes((tm, d), bf16) + _nbytes((d, n), bf16) + 2 * _nbytes((tm, n), bf16)
           + 8 * _nbytes((tm, chunk), f32))
    body = functools.partial(_xbc_body, chunk=chunk, tiles_per_batch=tiles_per_batch)
    return pl.pallas_call(
        body, out_shape=[SDS((m, n), bf16), SDS((nb * SUBLANES, n), f32)], grid=(m // tm,),
        in_specs=[pl.BlockSpec((tm, d), lambda i: (i, 0)), _resident((d, n)), _resident(conv_w.shape),
                  _resident(conv_b.shape)],
        out_specs=[pl.BlockSpec((tm, n), lambda i: (i, 0)),
                   pl.BlockSpec((SUBLANES, n), lambda i: (i // tiles_per_batch, 0))],
        scratch_shapes=[pltpu.VMEM((SUBLANES, n), f32), pltpu.VMEM((SUBLANES + tm, chunk), f32)],
        compiler_params=_params(("arbitrary",), est), name="xbc_proj")(h, w, conv_w, conv_b)


def _swa_prompt_body(sink_ref, q_ref, k_ref, v_ref, o_ref, bias_scr, *, nblk):
    blk = SWA_BLOCK

    @pl.when(pl.program_id(0) == 0)
    def _():
        row = lax.broadcasted_iota(jnp.int32, (blk, 2 * blk), 0)
        col = lax.broadcasted_iota(jnp.int32, (blk, 2 * blk), 1)
        dist = row + blk - col
        allowed = (dist >= 0) & (dist <= WINDOW)
        distf = dist.astype(f32)
        for hh in range(SWA_HEADS):
            slope = 2.0 ** (-8.0 * (hh + 1) / SWA_HEADS) * LOG2_E
            bias = jnp.where(allowed, -slope * distf, MASKED)
            bias_scr[1, hh] = bias
            bias_scr[0, hh] = jnp.where(col >= blk, bias, MASKED)

    lane = lax.broadcasted_iota(jnp.int32, (blk, LANES), 1)
    low_half = lane < SWA_HEAD_DIM
    zero = jnp.zeros((blk, LANES), bf16)

    def block(j, carry):
        r0 = pl.multiple_of(j * blk, blk)
        rp = pl.multiple_of(jnp.maximum(j - 1, 0) * blk, blk)
        first = jnp.minimum(j, 1)
        for g in range(SWA_KV_HEADS):
            ks = slice(g * LANES, (g + 1) * LANES)
            vs = slice(2 * g * LANES, 2 * (g + 1) * LANES)
            kcat = jnp.concatenate([k_ref[pl.ds(rp, blk), ks], k_ref[pl.ds(r0, blk), ks]], axis=0)
            vcat = jnp.concatenate([v_ref[pl.ds(rp, blk), vs], v_ref[pl.ds(r0, blk), vs]], axis=0)
            for pr in range(SWA_GROUP // 2):
                c0 = (g * SWA_GROUP + 2 * pr) * SWA_HEAD_DIM
                q2 = q_ref[pl.ds(r0, blk), c0:c0 + LANES]
                res = []
                for half in range(2):
                    r = 2 * pr + half
                    qm = jnp.where(low_half, q2, zero) if half == 0 else jnp.where(low_half, zero, q2)
                    s = lax.dot_general(qm, kcat, NT_DIMS, preferred_element_type=f32)
                    s = s + bias_scr[first, g * SWA_GROUP + r]
                    sink = sink_ref[g * SWA_GROUP + r]
                    mx = jnp.maximum(jnp.max(s, axis=-1, keepdims=True), sink)
                    p = jnp.exp2(s - mx)
                    o = jnp.dot(p.astype(bf16), vcat, preferred_element_type=f32)
                    es = jnp.exp2(sink - mx)
                    lo, hi = o[:, :LANES], o[:, LANES:]
                    res.append(lo / (hi + es) if half == 0 else hi / (lo + es))
                o_ref[pl.ds(r0, blk), c0:c0 + LANES] = jnp.where(low_half, res[0], res[1]).astype(o_ref.dtype)
        return carry

    lax.fori_loop(0, nblk, block, 0)


def _swa_prompt(q, kd, v4, sinks_log2, nbatch, t):
    nq, nk, nv = q.shape[1], kd.shape[1], v4.shape[1]
    bias_shape = (2, SWA_HEADS, SWA_BLOCK, 2 * SWA_BLOCK)
    est = (2 * (2 * _nbytes((t, nq), bf16) + _nbytes((t, nk), bf16) + _nbytes((t, nv), bf16)) + _nbytes(bias_shape, f32)
           + 32 * _nbytes((SWA_BLOCK, 2 * SWA_BLOCK), f32))
    body = functools.partial(_swa_prompt_body, nblk=t // SWA_BLOCK)
    rows = lambda width: pl.BlockSpec((t, width), lambda b: (b, 0))
    return pl.pallas_call(
        body, out_shape=SDS(q.shape, bf16), grid=(nbatch,),
        in_specs=[pl.BlockSpec(memory_space=pltpu.SMEM), rows(nq), rows(nk), rows(nv)],
        out_specs=rows(nq), scratch_shapes=[pltpu.VMEM(bias_shape, f32)],
        compiler_params=_params(("arbitrary",), est), name="swa_prompt")(sinks_log2, q, kd, v4)


def _swa_decode_body(q_ref, kn_ref, vn_ref, ck_ref, cv_ref, slope_ref, sink_ref, o_ref, *, bb, w_buf):
    pair_heads = 2 * SWA_GROUP
    lane = lax.broadcasted_iota(jnp.int32, (pair_heads, LANES), 1)
    rowi = lax.broadcasted_iota(jnp.int32, (pair_heads, LANES), 0)
    own_half = (lane < SWA_HEAD_DIM) == (rowi < SWA_GROUP)
    tok = lax.broadcasted_iota(jnp.int32, (pair_heads, w_buf), 1)
    distf = (w_buf - tok).astype(f32)
    for b in range(bb):
        for kp in range(SWA_KV_HEADS // 2):
            ls = slice(kp * LANES, (kp + 1) * LANES)
            hs = slice(kp * pair_heads, (kp + 1) * pair_heads)
            q8 = q_ref[b, hs, :].astype(bf16)
            k2 = ck_ref[b, :, ls].astype(bf16)
            v2 = cv_ref[b, :, ls].astype(bf16)
            kn = kn_ref[b:b + 1, ls].astype(bf16).astype(f32)
            vn = vn_ref[b:b + 1, ls].astype(bf16).astype(f32)
            slope = slope_ref[hs, :]
            sink = sink_ref[hs, 0:1]
            s = lax.dot_general(q8, k2, NT_DIMS, preferred_element_type=f32) - slope[:, :w_buf] * distf
            sn = jnp.sum(q8.astype(f32) * kn, axis=-1, keepdims=True)
            mx = jnp.maximum(jnp.maximum(jnp.max(s, axis=-1, keepdims=True), sn), sink)
            p = jnp.exp(s - mx)
            pn = jnp.exp(sn - mx)
            den = jnp.sum(p, axis=-1, keepdims=True) + pn + jnp.exp(sink - mx)
            o = jnp.dot(p.astype(bf16), v2, preferred_element_type=f32) + pn.astype(bf16).astype(f32) * vn
            o_ref[b, hs, :] = jnp.where(own_half, o / den, 0.0).astype(o_ref.dtype)


def _swa_decode(q_slot, k_new, v_new, cache_k, cache_v, slopes, sinks, bb):
    nb, w_buf, kvd = cache_k.shape
    est = 2 * (2 * _nbytes((bb, w_buf, kvd), f32) + 2 * _nbytes((bb, SWA_HEADS, LANES), f32)) + (1 << 20)
    body = functools.partial(_swa_decode_body, bb=bb, w_buf=w_buf)
    return pl.pallas_call(
        body, out_shape=SDS((nb, SWA_HEADS, LANES), bf16), grid=(nb // bb,),
        in_specs=[pl.BlockSpec((bb, SWA_HEADS, LANES), lambda i: (i, 0, 0)),
                  pl.BlockSpec((bb, kvd), lambda i: (i, 0)), pl.BlockSpec((bb, kvd), lambda i: (i, 0)),
                  pl.BlockSpec((bb, w_buf, kvd), lambda i: (i, 0, 0)), pl.BlockSpec((bb, w_buf, kvd), lambda i: (i, 0, 0)),
                  _resident((SWA_HEADS, LANES)), _resident((SWA_HEADS, LANES))],
        out_specs=pl.BlockSpec((bb, SWA_HEADS, LANES), lambda i: (i, 0, 0)),
        compiler_params=_params(("parallel",), est), name="swa_decode")(q_slot, k_new, v_new, cache_k, cache_v, slopes, sinks)


def _mem_attn_body(q_ref, k_ref, v_ref, o_ref, *, bb):
    for b in range(bb):
        for h in range(MEM_HEADS):
            hs = slice(h * MEM_HEAD_DIM, (h + 1) * MEM_HEAD_DIM)
            q = q_ref[b, :, hs]
            k = k_ref[b, :, hs].astype(bf16)
            v = v_ref[b, :, hs].astype(bf16)
            s = lax.dot_general(q, k, NT_DIMS, preferred_element_type=f32)
            p = jnp.exp(s - jnp.max(s, axis=-1, keepdims=True))
            den = jnp.sum(p, axis=-1, keepdims=True)
            o = jnp.dot(p.astype(bf16), v, preferred_element_type=f32)
            o_ref[b, :, hs] = (o / den).astype(o_ref.dtype)


def _mem_attn(q, k, v, bb, tq):
    nb, t, md = q.shape
    mt = k.shape[1]
    est = 2 * (2 * _nbytes((bb, tq, md), bf16) + 2 * _nbytes((bb, mt, md), f32)) + 4 * _nbytes((tq, mt), f32) + (2 << 20)
    body = functools.partial(_mem_attn_body, bb=bb)
    return pl.pallas_call(
        body, out_shape=SDS((nb, t, md), bf16), grid=(nb // bb, t // tq),
        in_specs=[pl.BlockSpec((bb, tq, md), lambda i, j: (i, j, 0)), pl.BlockSpec((bb, mt, md), lambda i, j: (i, 0, 0)),
                  pl.BlockSpec((bb, mt, md), lambda i, j: (i, 0, 0))],
        out_specs=pl.BlockSpec((bb, tq, md), lambda i, j: (i, j, 0)),
        compiler_params=_params(("parallel", "arbitrary"), est), name="mem_attn")(q, k, v)


def _ssd_gate_norm(y, z, gain):
    y = y * z
    gsz = y.shape[1] // SSD_GROUPS
    outs = []
    for g in range(SSD_GROUPS):
        yg = y[:, g * gsz:(g + 1) * gsz]
        ms = jnp.mean(yg * yg, axis=-1, keepdims=True)
        outs.append(yg * lax.rsqrt(ms + EPS) * gain[:, g * gsz:(g + 1) * gsz])
    return jnp.concatenate(outs, axis=-1)


def _ssd_prompt_body(xc_ref, z_ref, dt_ref, dtb_ref, alog_ref, dsk_ref, gn_ref, rexp_ref, ltri_ref,
                     y_ref, hout_ref, h_scr):
    c = pl.program_id(1)
    chunk = SSD_CHUNK
    di = z_ref.shape[1]
    gn_w = SSD_GROUPS * SSD_D_STATE
    gw = di // SSD_GROUPS

    @pl.when(c == 0)
    def _():
        h_scr[...] = jnp.zeros_like(h_scr)

    xs = xc_ref[:, :di].astype(f32)
    bm = xc_ref[:, di:di + gn_w]
    cm = xc_ref[:, di + gn_w:]

    dt = _softplus(dt_ref[...] + dtb_ref[...])
    da = dt * (-jnp.exp(alog_ref[...]))
    acs = jnp.dot(ltri_ref[...], da, precision=lax.Precision.HIGHEST, preferred_element_type=f32)
    acs_t = acs.T
    dt_t = dt.T
    rexp = rexp_ref[...]
    eacs_e = _split_dot(jnp.exp(acs), rexp)
    xw = (xs * _split_dot(jnp.exp(acs[chunk - 1:chunk, :] - acs) * dt, rexp)).astype(bf16)

    row = lax.broadcasted_iota(jnp.int32, (chunk, chunk), 0)
    col = lax.broadcasted_iota(jnp.int32, (chunk, chunk), 1)
    causal = row >= col
    low_half = col < SSD_HEAD_DIM
    zero = jnp.zeros((chunk, LANES), bf16)
    heads_per_group = SSD_HEADS // SSD_GROUPS
    ys = []
    for g in range(SSD_GROUPS):
        ns = slice(g * SSD_D_STATE, (g + 1) * SSD_D_STATE)
        ls = slice(g * gw, (g + 1) * gw)
        cb = lax.dot_general(cm[:, ns], bm[:, ns], NT_DIMS, preferred_element_type=f32)
        hg = h_scr[:, ls]
        y_off = jnp.dot(cm[:, ns], hg.astype(bf16), preferred_element_type=f32) * eacs_e[:, ls]
        y_diag = []
        for pr in range(heads_per_group // 2):
            h0 = g * heads_per_group + 2 * pr
            xp = xc_ref[:, h0 * SSD_HEAD_DIM:h0 * SSD_HEAD_DIM + LANES]
            ws = []
            for hh in (h0, h0 + 1):
                diff = acs[:, hh:hh + 1] - acs_t[hh:hh + 1, :]
                ws.append((cb * jnp.exp(jnp.where(causal, diff, MASKED)) * dt_t[hh:hh + 1, :]).astype(bf16))
            x2 = jnp.concatenate([jnp.where(low_half, xp, zero), jnp.where(low_half, zero, xp)], axis=0)
            y_diag.append(jnp.dot(jnp.concatenate(ws, axis=1), x2, preferred_element_type=f32))
        st = lax.dot_general(bm[:, ns], xw[:, ls], TN_DIMS, preferred_element_type=f32)
        h_scr[:, ls] = eacs_e[chunk - 1:chunk, ls] * hg + st
        ys.append(jnp.concatenate(y_diag, axis=-1) + y_off)
    y = jnp.concatenate(ys, axis=-1) + dsk_ref[...] * xs
    y_ref[...] = _ssd_gate_norm(y, z_ref[...].astype(f32), gn_ref[...]).astype(y_ref.dtype)

    @pl.when(c == pl.num_programs(1) - 1)
    def _():
        hout_ref[0] = h_scr[...].T


def _ssd_prompt(xc, z, dt, ssd_w, nbatch, t):
    _, _, dtb, alog, dsk, gn, rexp, ltri = ssd_w
    ch, di = xc.shape[1], z.shape[1]
    nchunk = t // SSD_CHUNK
    rows = lambda b, c: (b * nchunk + c, 0)
    est = (2 * (_nbytes((SSD_CHUNK, ch), bf16) + 2 * _nbytes((SSD_CHUNK, di), bf16) + _nbytes((di, SSD_D_STATE), f32))
           + _nbytes((SSD_D_STATE, di), f32) + 24 * _nbytes((SSD_CHUNK, ch), f32))
    return pl.pallas_call(
        _ssd_prompt_body,
        out_shape=[SDS((nbatch * t, di), bf16), SDS((nbatch, di, SSD_D_STATE), f32)],
        grid=(nbatch, nchunk),
        in_specs=[pl.BlockSpec((SSD_CHUNK, ch), rows), pl.BlockSpec((SSD_CHUNK, di), rows),
                  pl.BlockSpec((SSD_CHUNK, LANES), rows),
                  _resident(dtb.shape), _resident(alog.shape),
                  _resident(dsk.shape), _resident(gn.shape), _resident(rexp.shape), _resident(ltri.shape)],
        out_specs=[pl.BlockSpec((SSD_CHUNK, di), rows), pl.BlockSpec((1, di, SSD_D_STATE), lambda b, c: (b, 0, 0))],
        scratch_shapes=[pltpu.VMEM((SSD_D_STATE, di), f32)],
        compiler_params=_params(("parallel", "arbitrary"), est), name="ssd_prompt")(xc, z, dt, dtb, alog, dsk, gn, rexp, ltri)


def _ssd_step_body(xbc_ref, z_ref, dt_ref, cs_ref, h_ref, cw_ref, cb_ref, dtb_ref, alog_ref, dsk_ref, gn_ref, rexp_ref,
                   y_ref, cso_ref, ho_ref, xdt_scr, da_scr, bm_scr, cm_scr, y_scr, *, bb):
    ch = xbc_ref.shape[1]
    di = z_ref.shape[1]
    gn_w = SSD_GROUPS * SSD_D_STATE
    gw = di // SSD_GROUPS
    halo = SSD_CONV - 1
    xr = xbc_ref[...]
    conv = cb_ref[...] + cw_ref[halo:halo + 1, :] * xr
    for k in range(halo):
        conv = conv + cw_ref[k:k + 1, :] * cs_ref[:, k * ch:(k + 1) * ch]
    for k in range(1, halo):
        cso_ref[:, (k - 1) * ch:k * ch] = cs_ref[:, k * ch:(k + 1) * ch]
    cso_ref[:, (halo - 1) * ch:] = xr
    xc = _silu(conv)
    xs = xc[:, :di]
    dt = _softplus(dt_ref[...] + dtb_ref[...])
    rexp = rexp_ref[...]
    xdt_scr[...] = xs * _split_dot(dt, rexp)
    da_scr[...] = _split_dot(jnp.exp(dt * (-jnp.exp(alog_ref[...]))), rexp)
    bm_scr[...] = xc[:, di:di + gn_w]
    cm_scr[...] = xc[:, di + gn_w:]

    rows = 2 * SUBLANES
    rowi = lax.broadcasted_iota(jnp.int32, (rows, di), 0)
    grp = lax.broadcasted_iota(jnp.int32, (rows, di), 1) // gw
    rown = lax.broadcasted_iota(jnp.int32, (rows, SSD_D_STATE), 0)
    ones_rows = jnp.where((rown == SSD_GROUPS) | (rown == SSD_GROUPS + 1), 1.0, 0.0).astype(f32)

    def per_row(b, carry):
        xrow = xdt_scr[pl.ds(b, 1), :]
        drow = da_scr[pl.ds(b, 1), :]
        d_hi = drow.astype(bf16).astype(f32)
        lhs = jnp.where(rowi == grp, xrow, jnp.where(rowi == SSD_GROUPS, d_hi, jnp.where(rowi == SSD_GROUPS + 1, drow - d_hi, 0.0)))
        brow = bm_scr[pl.ds(b, 1), :]
        crow = cm_scr[pl.ds(b, 1), :]
        rhs_b = jnp.zeros((rows, SSD_D_STATE), f32)
        c_rows = jnp.zeros((rows, SSD_D_STATE), f32)
        for g in range(SSD_GROUPS):
            ns = slice(g * SSD_D_STATE, (g + 1) * SSD_D_STATE)
            rhs_b = jnp.where(rown == g, brow[:, ns], rhs_b)
            c_rows = jnp.where(rown == g, crow[:, ns], c_rows)
        rhs = jnp.concatenate([rhs_b, ones_rows], axis=1).astype(bf16)
        sd = lax.dot_general(lhs.astype(bf16), rhs, TN_DIMS, preferred_element_type=f32)
        hn = sd[:, SSD_D_STATE:] * h_ref[b] + sd[:, :SSD_D_STATE]
        ho_ref[b] = hn
        y8 = lax.dot_general(c_rows.astype(bf16), hn.astype(bf16), NT_DIMS, preferred_element_type=f32)
        y_scr[pl.ds(b, 1), :] = jnp.sum(jnp.where(rowi == grp, y8, 0.0), axis=0, keepdims=True)
        return carry

    lax.fori_loop(0, bb, per_row, 0)
    y = y_scr[...] + dsk_ref[...] * xs
    y_ref[...] = _ssd_gate_norm(y, _silu(z_ref[...]), gn_ref[...]).astype(y_ref.dtype)


def _ssd_step(xbc, z, dt, conv_state, h0, ssd_w, bb):
    cw, cb, dtb, alog, dsk, gn, rexp, _ = ssd_w
    nb, ch = xbc.shape
    di = z.shape[1]
    gn_w = SSD_GROUPS * SSD_D_STATE
    halo = SSD_CONV - 1
    est = (2 * (2 * _nbytes((bb, di, SSD_D_STATE), f32) + 2 * _nbytes((bb, halo * ch), f32) + 4 * _nbytes((bb, ch), f32))
           + 8 * _nbytes((di, 2 * SSD_D_STATE), f32))
    body = functools.partial(_ssd_step_body, bb=bb)
    r2 = lambda i: (i, 0)
    return pl.pallas_call(
        body,
        out_shape=[SDS((nb, di), bf16), SDS((nb, halo * ch), f32), SDS((nb, di, SSD_D_STATE), f32)],
        grid=(nb // bb,),
        in_specs=[pl.BlockSpec((bb, ch), r2), pl.BlockSpec((bb, di), r2), pl.BlockSpec((bb, LANES), r2),
                  pl.BlockSpec((bb, halo * ch), r2), pl.BlockSpec((bb, di, SSD_D_STATE), lambda i: (i, 0, 0)),
                  _resident(cw.shape), _resident(cb.shape), _resident(dtb.shape), _resident(alog.shape),
                  _resident(dsk.shape), _resident(gn.shape), _resident(rexp.shape)],
        out_specs=[pl.BlockSpec((bb, di), r2), pl.BlockSpec((bb, halo * ch), r2),
                   pl.BlockSpec((bb, di, SSD_D_STATE), lambda i: (i, 0, 0))],
        scratch_shapes=[pltpu.VMEM((bb, di), f32), pltpu.VMEM((bb, di), f32), pltpu.VMEM((bb, gn_w), f32),
                        pltpu.VMEM((bb, gn_w), f32), pltpu.VMEM((bb, di), f32)],
        compiler_params=_params(("parallel",), est), name="ssd_step")(xbc, z, dt, conv_state, h0, cw, cb, dtb, alog, dsk, gn, rexp)


def _merge_body(a_ref, s_ref, m_ref, g0_ref, g1_ref, g2_ref, wa_ref, ws_ref, wm_ref, o_ref, *, chunk):
    a, s, m = a_ref[...], s_ref[...], m_ref[...]
    for c0 in range(0, o_ref.shape[1], chunk):
        cs = slice(c0, c0 + chunk)
        acc = g0_ref[:, cs].astype(f32) * jnp.dot(a, wa_ref[:, cs], preferred_element_type=f32)
        acc = acc + g1_ref[:, cs].astype(f32) * jnp.dot(s, ws_ref[:, cs], preferred_element_type=f32)
        acc = acc + g2_ref[:, cs].astype(f32) * jnp.dot(m, wm_ref[:, cs], preferred_element_type=f32)
        o_ref[:, cs] = acc.astype(o_ref.dtype)


def _merge(a, s, mo, gates, wa, ws, wm, tm):
    m, d = s.shape[0], wa.shape[1]
    chunk = 512
    est = (2 * (_nbytes((tm, a.shape[1]), bf16) + _nbytes((tm, s.shape[1]), bf16) + _nbytes((tm, mo.shape[1]), bf16)
                + 4 * _nbytes((tm, d), bf16))
           + _nbytes(wa.shape, bf16) + _nbytes(ws.shape, bf16) + _nbytes(wm.shape, bf16) + 6 * _nbytes((tm, chunk), f32))
    body = functools.partial(_merge_body, chunk=chunk)
    return pl.pallas_call(
        body, out_shape=SDS((m, d), bf16), grid=(m // tm,),
        in_specs=[pl.BlockSpec((tm, a.shape[1]), lambda i: (i, 0)), pl.BlockSpec((tm, s.shape[1]), lambda i: (i, 0)),
                  pl.BlockSpec((tm, mo.shape[1]), lambda i: (i, 0)),
                  pl.BlockSpec((tm, d), lambda i: (i, 0)), pl.BlockSpec((tm, d), lambda i: (i, 1)),
                  pl.BlockSpec((tm, d), lambda i: (i, 2)),
                  _resident(wa.shape), _resident(ws.shape), _resident(wm.shape)],
        out_specs=pl.BlockSpec((tm, d), lambda i: (i, 0)),
        compiler_params=_params(("parallel",), est), name="merge")(a, s, mo, gates, gates, gates, wa, ws, wm)


def _ffn_body(x_ref, g_ref, wg_ref, wu_ref, wd_ref, o_ref, h_scr):
    @pl.when(pl.program_id(1) == 0)
    def _():
        x = x_ref[...]
        ms = jnp.mean(x * x, axis=-1, keepdims=True)
        h_scr[...] = (x * lax.rsqrt(ms + EPS) * g_ref[...]).astype(bf16)
        o_ref[...] = x

    h = h_scr[...]
    gate = jnp.dot(h, wg_ref[...], preferred_element_type=f32)
    up = jnp.dot(h, wu_ref[...], preferred_element_type=f32)
    act = (_silu(gate) * up).astype(bf16)
    chunk = 512
    for n0 in range(0, o_ref.shape[1], chunk):
        o_ref[:, n0:n0 + chunk] += jnp.dot(act, wd_ref[:, n0:n0 + chunk], preferred_element_type=f32)


def _ffn(x, gain, wg, wu, wd, tm, tf):
    m, d = x.shape
    dff = wg.shape[1]
    est = (3 * _nbytes((tm, d), f32) + _nbytes((tm, d), bf16) + 2 * 3 * _nbytes((d, tf), bf16)
           + 4 * _nbytes((tm, tf), f32))
    return pl.pallas_call(
        _ffn_body, out_shape=SDS((m, d), f32), grid=(m // tm, dff // tf),
        in_specs=[pl.BlockSpec((tm, d), lambda i, c: (i, 0), pipeline_mode=pl.Buffered(1)), _resident((1, d)),
                  pl.BlockSpec((d, tf), lambda i, c: (0, c)), pl.BlockSpec((d, tf), lambda i, c: (0, c)),
                  pl.BlockSpec((tf, d), lambda i, c: (c, 0))],
        out_specs=pl.BlockSpec((tm, d), lambda i, c: (i, 0)),
        scratch_shapes=[pltpu.VMEM((tm, d), bf16)],
        compiler_params=_params(("parallel", "arbitrary"), est), name="ffn")(x, gain.reshape(1, d), wg, wu, wd)


def _dup_heads(w):
    lead = w.shape[:-1]
    w = w.reshape(*lead, SWA_KV_HEADS, 1, SWA_HEAD_DIM)
    return jnp.broadcast_to(w, (*lead, SWA_KV_HEADS, 2, SWA_HEAD_DIM)).reshape(*lead, SWA_KV_HEADS * LANES)


def _slot_heads(w, axis):
    w = jnp.moveaxis(w, axis, -1)
    lead = w.shape[:-1]
    w = w.reshape(*lead, SWA_HEADS, 1, SWA_HEAD_DIM)
    kv_half = (jnp.arange(SWA_HEADS) // SWA_GROUP) % 2
    sel = (kv_half[:, None] == jnp.arange(2)[None, :]).astype(w.dtype)[:, :, None]
    w = (w * sel).reshape(*lead, SWA_HEADS * LANES)
    return jnp.moveaxis(w, -1, axis)


def _pad_lanes(v, n=LANES):
    return jnp.pad(v, [(0, 0)] * (v.ndim - 1) + [(0, n - v.shape[-1])])


def _layer(xp, xs, cache_k, cache_v, cmem_k, cmem_v, state_ssm, state_conv, mem_prompt, w):
    bp, tp, d = xp.shape
    bs = xs.shape[0]
    w_buf = cache_k.shape[1]
    assert xs.shape[1] == 1 and w_buf == WINDOW and tp % SSD_CHUNK == 0 and bs % SUBLANES == 0
    q_dim = SWA_HEADS * SWA_HEAD_DIM
    kv_dim = SWA_KV_HEADS * SWA_HEAD_DIM
    di = SSD_HEADS * SSD_HEAD_DIM
    ch = di + 2 * SSD_GROUPS * SSD_D_STATE
    md = MEM_HEADS * MEM_HEAD_DIM
    o = 0
    w_in = w["w_in"].astype(bf16)
    wq, o = w_in[:, o:o + q_dim], o + q_dim
    wk, o = w_in[:, o:o + kv_dim], o + kv_dim
    wv, o = w_in[:, o:o + kv_dim], o + kv_dim
    wz, o = w_in[:, o:o + di], o + di
    wxbc, o = w_in[:, o:o + ch], o + ch
    wdt, o = w_in[:, o:o + SSD_HEADS], o + SSD_HEADS
    wqm, o = w_in[:, o:o + md], o + md
    wg = w_in[:, o:]
    wdt = _pad_lanes(wdt)

    q_scale = SWA_HEAD_DIM ** -0.5
    gq = jnp.tile(w["q_norm_swa"], SWA_HEADS) * q_scale
    gk = jnp.tile(w["k_norm_swa"], SWA_KV_HEADS)
    head_of = jnp.arange(2 * LANES) // SWA_HEAD_DIM
    blockdiag = (head_of[:, None] == head_of[None, :]).astype(bf16)
    w_qkv_p = jnp.concatenate([wq, _dup_heads(wk), _dup_heads(wv), wdt], axis=1)
    w_qkv_s = jnp.concatenate([_slot_heads(wq, 1), wk, wv, wdt], axis=1)
    gqm = (jnp.tile(w["q_norm_mem"], MEM_HEADS) * MEM_HEAD_DIM ** -0.5).reshape(1, md)
    gkm = jnp.tile(w["k_norm_mem"], MEM_HEADS).reshape(1, md)
    wz_b, wxbc_b, wqm_b, wg_b = wz, wxbc, wqm, wg
    w_mem_k, w_mem_v = w["w_mem_kv"][:, :md].astype(bf16), w["w_mem_kv"][:, md:].astype(bf16)
    wa, ws, wm = w["w_up_swa"].astype(bf16), w["w_up_ssd"].astype(bf16), w["w_up_mem"].astype(bf16)
    wa_slot = _slot_heads(w["w_up_swa"], 0).astype(bf16)
    w_out = w["w_out"].astype(bf16)
    w_gate, w_up, w_down = w["w_gate"].astype(bf16), w["w_up"].astype(bf16), w["w_down"].astype(bf16)

    head_of_lane = jnp.arange(di) // SSD_HEAD_DIM
    rexp = (jnp.arange(LANES)[:, None] == head_of_lane[None, :]).astype(bf16)
    ltri = (jnp.arange(SSD_CHUNK)[:, None] >= jnp.arange(SSD_CHUNK)[None, :]).astype(f32)
    ssd_w = (w["conv_w"], w["conv_b"].reshape(1, ch), _pad_lanes(w["dt_bias"].reshape(1, -1)),
             _pad_lanes(w["a_log"].reshape(1, -1)), jnp.repeat(w["d_skip"], SSD_HEAD_DIM).reshape(1, di),
             w["ssd_norm"].reshape(1, di), rexp, ltri)
    slopes = jnp.broadcast_to(jnp.exp2(-8.0 * jnp.arange(1, SWA_HEADS + 1, dtype=f32) / SWA_HEADS)[:, None], (SWA_HEADS, LANES))
    sinks_b = jnp.broadcast_to(w["swa_sinks"].astype(f32)[:, None], (SWA_HEADS, LANES))

    def finish(x2, a_out, s_out, m_out, gates, wa_, tm, tm_ffn):
        merged = _merge(a_out, s_out, m_out, gates, wa_, ws, wm, tm)
        x1 = _proj(merged, w_out, _epi_residual, f32, tm, d, aux=(x2,), aux_specs=(pl.BlockSpec((tm, d), lambda i, j: (i, 0)),),
                   name="out_proj")
        return _ffn(x1, w["norm_ffn"], w_gate, w_up, w_down, tm_ffn, 512)

    mp = bp * tp
    x2 = xp.reshape(mp, d)
    tm = min(1024, tp)
    tpb = tp // tm
    h = _rmsnorm(x2, w["norm_mix"], min(512, tp))
    q, kd, v4, dt, klast, vlast = _qkv_proj(
        h, w_qkv_p, (gq * LOG2_E).reshape(1, -1), _dup_heads(gk).reshape(1, -1), blockdiag,
        q_dim, 2 * kv_dim, 2 * kv_dim, tm, tpb, bf16, bf16, True)
    z = _proj(h, wz_b, _epi_silu, bf16, tm, di, name="z_proj")
    xc, xtail = _xbc_proj(h, wxbc_b, ssd_w[0], ssd_w[1], tm, tpb)
    qm = _proj(h, wqm_b, _epi_headnorm256, bf16, tm, md, aux=(gqm,), aux_specs=(_resident((1, md)),), name="qm_proj")
    gates = _proj(h, wg_b, _epi_sigmoid, bf16, tm, d, name="gate_proj")
    a_out = _swa_prompt(q, kd, v4, w["swa_sinks"].astype(f32) * LOG2_E, bp, tp)
    s_out, p_h = _ssd_prompt(xc, z, dt, ssd_w, bp, tp)
    mt = mem_prompt.shape[1]
    hm = _rmsnorm(mem_prompt.reshape(bp * mt, d), w["norm_mem"], mt)
    mk = _proj(hm, w_mem_k, _epi_headnorm256, f32, mt, md, aux=(gkm,), aux_specs=(_resident((1, md)),), name="mem_k_proj")
    mv = _proj(hm, w_mem_v, _epi_plain, f32, mt, md, name="mem_v_proj")
    m_out = _mem_attn(qm.reshape(bp, tp, md), mk.reshape(bp, mt, md), mv.reshape(bp, mt, md), 1, tm)
    yp = finish(x2, a_out, s_out, m_out.reshape(mp, md), gates, wa, min(512, tp), tm).reshape(bp, tp, d)

    undup = lambda t: t.reshape(bp, WINDOW, SWA_KV_HEADS, 2, SWA_HEAD_DIM)[:, :, :, 0, :]
    p_k, p_v = undup(klast), undup(vlast)
    p_mk = mk.reshape(bp, mt, MEM_HEADS, MEM_HEAD_DIM)
    p_mv = mv.reshape(bp, mt, MEM_HEADS, MEM_HEAD_DIM)
    p_h = p_h.reshape(bp, SSD_HEADS, SSD_HEAD_DIM, SSD_D_STATE)
    p_c = xtail.reshape(bp, SUBLANES, ch)[:, SUBLANES - (SSD_CONV - 1):, :]

    xs2 = xs.reshape(bs, d)
    hs = _rmsnorm(xs2, w["norm_mix"], bs)
    qs, ks, vs, dts = _qkv_proj(hs, w_qkv_s, _slot_heads(gq, 0).reshape(1, -1), gk.reshape(1, -1), blockdiag,
                                SWA_HEADS * LANES, kv_dim, kv_dim, bs, 1, f32, f32, False)
    zs = _proj(hs, wz_b, _epi_plain, f32, bs, di, name="z_proj_s")
    xbcs = _proj(hs, wxbc_b, _epi_plain, f32, bs, ch, name="xbc_proj_s")
    qms = _proj(hs, wqm_b, _epi_headnorm256, bf16, bs, md, aux=(gqm,), aux_specs=(_resident((1, md)),), name="qm_proj_s")
    gates_s = _proj(hs, wg_b, _epi_sigmoid, bf16, bs, d, name="gate_proj_s")
    ck = cache_k.reshape(bs, w_buf, kv_dim)
    cv = cache_v.reshape(bs, w_buf, kv_dim)
    a_s = _swa_decode(qs.reshape(bs, SWA_HEADS, LANES), ks, vs, ck, cv, slopes, sinks_b, SUBLANES)
    halo = SSD_CONV - 1
    s_s, s_c, s_h = _ssd_step(xbcs, zs, dts, state_conv.reshape(bs, halo * ch), state_ssm.reshape(bs, di, SSD_D_STATE),
                              ssd_w, SUBLANES)
    qm8 = jnp.broadcast_to(qms[:, None, :], (bs, 2 * SUBLANES, md))
    m_s = _mem_attn(qm8, cmem_k.reshape(bs, -1, md), cmem_v.reshape(bs, -1, md), 4, 2 * SUBLANES)[:, 0, :]
    ys = finish(xs2, a_s.reshape(bs, SWA_HEADS * LANES), s_s, m_s, gates_s, wa_slot, bs, bs).reshape(bs, 1, d)

    s_k = jnp.concatenate([ck[:, 1:], ks[:, None, :]], axis=1).reshape(bs, w_buf, SWA_KV_HEADS, SWA_HEAD_DIM)
    s_v = jnp.concatenate([cv[:, 1:], vs[:, None, :]], axis=1).reshape(bs, w_buf, SWA_KV_HEADS, SWA_HEAD_DIM)
    s_h = s_h.reshape(bs, SSD_HEADS, SSD_HEAD_DIM, SSD_D_STATE)
    s_c = s_c.reshape(bs, halo, ch)
    return yp, ys, (p_k, p_v, p_mk, p_mv, p_h, p_c), (s_k, s_v, s_h, s_c)


_WEIGHT_NAMES = ("norm_mix", "w_in", "q_norm_swa", "k_norm_swa", "swa_sinks", "conv_w", "conv_b", "dt_bias", "a_log",
                 "d_skip", "ssd_norm", "norm_mem", "w_mem_kv", "q_norm_mem", "k_norm_mem", "w_up_swa", "w_up_ssd",
                 "w_up_mem", "w_out", "norm_ffn", "w_gate", "w_up", "w_down")


def kernel(x_prompt, x_sample, cache_swa_k, cache_swa_v, cache_mem_k, cache_mem_v, state_ssm, state_conv, mem_prompt, norm_mix, w_in, q_norm_swa, k_norm_swa, swa_sinks, conv_w, conv_b, dt_bias, a_log, d_skip, ssd_norm, norm_mem, w_mem_kv, q_norm_mem, k_norm_mem, w_up_swa, w_up_ssd, w_up_mem, w_out, norm_ffn, w_gate, w_up, w_down):
    weights = (norm_mix, w_in, q_norm_swa, k_norm_swa, swa_sinks, conv_w, conv_b, dt_bias, a_log, d_skip, ssd_norm,
               norm_mem, w_mem_kv, q_norm_mem, k_norm_mem, w_up_swa, w_up_ssd, w_up_mem, w_out, norm_ffn, w_gate, w_up, w_down)
    depth = w_in.shape[0]
    layer = (lambda a, l: a.reshape(a.shape[1:])) if depth == 1 else (lambda a, l: a[l])
    yp, ys = x_prompt, x_sample
    p_outs, s_outs = [], []
    for l in range(depth):
        w = {n: layer(a, l) for n, a in zip(_WEIGHT_NAMES, weights)}
        yp, ys, po, so = _layer(yp, ys, layer(cache_swa_k, l), layer(cache_swa_v, l), layer(cache_mem_k, l),
                                layer(cache_mem_v, l), layer(state_ssm, l), layer(state_conv, l), mem_prompt, w)
        p_outs.append(po)
        s_outs.append(so)
    stack = lambda outs, i: jnp.stack([o[i] for o in outs])
    return (yp, ys, *(stack(p_outs, i) for i in range(6)), *(stack(s_outs, i) for i in range(4)))
```

```python
import functools

import jax
import jax.numpy as jnp
from jax import lax
from jax.experimental import pallas as pl
from jax.experimental.pallas import tpu as pltpu

f32 = jnp.float32
bf16 = jnp.bfloat16
SDS = jax.ShapeDtypeStruct

LANES = 128
SUBLANES = 8
VMEM_BYTES_V7X = 64 * 1024 * 1024
VMEM_HEADROOM = 8 * 1024 * 1024

EPS = 1e-6
SWA_HEADS = 16
SWA_KV_HEADS = 4
SWA_GROUP = SWA_HEADS // SWA_KV_HEADS
SWA_HEAD_DIM = 64
WINDOW = 128
SWA_BLOCK = 128
SSD_HEAD_DIM = 64
SSD_HEADS = 32
SSD_GROUPS = 4
SSD_D_STATE = 128
SSD_CONV = 4
SSD_CHUNK = 128
MEM_HEADS = 4
MEM_HEAD_DIM = 256
N_BRANCH = 3
MASKED = -1e30
LOG2_E = 1.4426950408889634

NT_DIMS = (((1,), (1,)), ((), ()))
TN_DIMS = (((0,), (0,)), ((), ()))


def _params(semantics, block_bytes):
    limit = min(int(block_bytes) + VMEM_HEADROOM, VMEM_BYTES_V7X - VMEM_HEADROOM)
    return pltpu.CompilerParams(dimension_semantics=semantics, vmem_limit_bytes=limit)


def _nbytes(shape, dtype):
    n = 1
    for s in shape:
        n *= s
    return n * jnp.dtype(dtype).itemsize


def _resident(shape):
    nd = len(shape)
    return pl.BlockSpec(shape, lambda *_: (0,) * nd, pipeline_mode=pl.Buffered(1))


def _sigmoid(x):
    return 0.5 + 0.5 * jnp.tanh(0.5 * x)


def _silu(x):
    hx = 0.5 * x
    return hx + hx * jnp.tanh(hx)


def _softplus(x):
    return jnp.maximum(x, 0.0) + jnp.log1p(jnp.exp(-jnp.abs(x)))


def _split_dot(v, m):
    hi = v.astype(bf16)
    lo = (v - hi.astype(f32)).astype(bf16)
    return jnp.dot(hi, m, preferred_element_type=f32) + jnp.dot(lo, m, preferred_element_type=f32)


def _headnorm64(acc, gain, blockdiag):
    outs = []
    wide = blockdiag.shape[0]
    for c in range(acc.shape[1] // wide):
        a = acc[:, c * wide:(c + 1) * wide]
        ss = jnp.dot((a * a).astype(bf16), blockdiag, preferred_element_type=f32)
        outs.append(a * lax.rsqrt(ss * (1.0 / SWA_HEAD_DIM) + EPS) * gain[:, c * wide:(c + 1) * wide])
    return outs[0] if len(outs) == 1 else jnp.concatenate(outs, axis=-1)


def _norm_body(x_ref, g_ref, o_ref):
    x = x_ref[...]
    ms = jnp.mean(x * x, axis=-1, keepdims=True)
    o_ref[...] = (x * lax.rsqrt(ms + EPS) * g_ref[...]).astype(o_ref.dtype)


def _rmsnorm(x, gain, tm):
    m, d = x.shape
    est = 2 * (_nbytes((tm, d), f32) + _nbytes((tm, d), bf16)) + _nbytes((tm, d), f32)
    return pl.pallas_call(
        _norm_body, out_shape=SDS((m, d), bf16), grid=(m // tm,),
        in_specs=[pl.BlockSpec((tm, d), lambda i: (i, 0)), _resident((1, d))],
        out_specs=pl.BlockSpec((tm, d), lambda i: (i, 0)),
        compiler_params=_params(("parallel",), est), name="rmsnorm")(x, gain.reshape(1, d))


def _qkv_body(x_ref, gx_ref, w_ref, gq_ref, gk_ref, bd_ref, h_ref, q_ref, k_ref, *rest, nq, nk, nv, chunk, prompt):
    v_ref, dt_ref, *last_refs = rest
    x = x_ref[...]
    h = (x * lax.rsqrt(jnp.mean(x * x, axis=-1, keepdims=True) + EPS) * gx_ref[...]).astype(bf16)
    h_ref[...] = h
    bd = bd_ref[...]
    tm = h.shape[0]
    for c0 in range(0, nq, chunk):
        acc = jnp.dot(h, w_ref[:, c0:c0 + chunk], preferred_element_type=f32)
        q_ref[:, c0:c0 + chunk] = _headnorm64(acc, gq_ref[:, c0:c0 + chunk], bd).astype(q_ref.dtype)
    for c0 in range(0, nk, chunk):
        w = min(chunk, nk - c0)
        acc = jnp.dot(h, w_ref[:, nq + c0:nq + c0 + w], preferred_element_type=f32)
        kn = _headnorm64(acc, gk_ref[:, c0:c0 + w], bd)
        k_ref[:, c0:c0 + w] = kn.astype(k_ref.dtype)
        if last_refs:
            last_refs[0][:, c0:c0 + w] = kn[tm - WINDOW:, :]
    for c0 in range(0, nv, chunk):
        w = min(chunk, nv - c0)
        acc = jnp.dot(h, w_ref[:, nq + nk + c0:nq + nk + c0 + w], preferred_element_type=f32)
        if prompt:
            low_half = lax.broadcasted_iota(jnp.int32, (tm, LANES), 1) < SWA_HEAD_DIM
            for t0 in range(0, w, LANES):
                a = acc[:, t0:t0 + LANES]
                v_ref[:, 2 * (c0 + t0):2 * (c0 + t0) + LANES] = jnp.where(low_half, a, 1.0).astype(v_ref.dtype)
                v_ref[:, 2 * (c0 + t0) + LANES:2 * (c0 + t0 + LANES)] = jnp.where(low_half, 1.0, a).astype(v_ref.dtype)
            last_refs[1][:, c0:c0 + w] = acc[tm - WINDOW:, :]
        else:
            v_ref[:, c0:c0 + w] = acc.astype(v_ref.dtype)
    dt_ref[...] = jnp.dot(h, w_ref[:, nq + nk + nv:], preferred_element_type=f32)


def _qkv_proj(x, gx, w, gq, gk, blockdiag, nq, nk, nv, tm, tiles_per_batch, q_dtype, kv_dtype, prompt):
    m, d = x.shape
    n = w.shape[1]
    chunk = 512
    rows = lambda width: pl.BlockSpec((tm, width), lambda i: (i, 0))
    v_width = 2 * nv if prompt else nv
    out_shape = [SDS((m, d), bf16), SDS((m, nq), q_dtype), SDS((m, nk), kv_dtype), SDS((m, v_width), kv_dtype),
                 SDS((m, LANES), f32)]
    out_specs = [rows(d), rows(nq), rows(nk), rows(v_width), rows(LANES)]
    if prompt:
        nb = m // (tm * tiles_per_batch)
        out_shape += [SDS((nb * WINDOW, nk), f32), SDS((nb * WINDOW, nv), f32)]
        out_specs += [pl.BlockSpec((WINDOW, nk), lambda i: (i // tiles_per_batch, 0)),
                      pl.BlockSpec((WINDOW, nv), lambda i: (i // tiles_per_batch, 0))]
    est = (2 * _nbytes((tm, d), f32) + 3 * _nbytes((tm, d), bf16) + _nbytes((d, n), bf16)
           + 2 * _nbytes((tm, nq + nk + 2 * nv), f32) + 4 * _nbytes((tm, chunk), f32))
    body = functools.partial(_qkv_body, nq=nq, nk=nk, nv=nv, chunk=chunk, prompt=prompt)
    return pl.pallas_call(
        body, out_shape=out_shape, grid=(m // tm,),
        in_specs=[pl.BlockSpec((tm, d), lambda i: (i, 0)), _resident((1, d)), _resident((d, n)), _resident((1, nq)),
                  _resident((1, nk)), _resident(blockdiag.shape)],
        out_specs=out_specs, compiler_params=_params(("arbitrary",), est), name="qkv_proj")(x, gx, w, gq, gk, blockdiag)


def _proj_body(h_ref, w_ref, *refs, epilogue, chunk, n_aux):
    aux, outs = refs[:n_aux], refs[n_aux:]
    h = h_ref[...]
    tn = w_ref.shape[1]
    for c0 in range(0, tn, chunk):
        acc = jnp.dot(h, w_ref[:, c0:c0 + chunk], preferred_element_type=f32)
        epilogue(acc, c0, chunk, aux, outs)


def _epi_plain(acc, c0, w, aux, outs):
    outs[0][:, c0:c0 + w] = acc.astype(outs[0].dtype)


def _epi_silu(acc, c0, w, aux, outs):
    outs[0][:, c0:c0 + w] = _silu(acc).astype(outs[0].dtype)


def _epi_sigmoid(acc, c0, w, aux, outs):
    outs[0][:, c0:c0 + w] = _sigmoid(acc).astype(outs[0].dtype)


def _epi_headnorm256(acc, c0, w, aux, outs):
    gain = aux[0]
    for h0 in range(0, w, MEM_HEAD_DIM):
        a = acc[:, h0:h0 + MEM_HEAD_DIM]
        ms = jnp.mean(a * a, axis=-1, keepdims=True)
        y = a * lax.rsqrt(ms + EPS) * gain[:, c0 + h0:c0 + h0 + MEM_HEAD_DIM]
        outs[0][:, c0 + h0:c0 + h0 + MEM_HEAD_DIM] = y.astype(outs[0].dtype)


def _epi_residual(acc, c0, w, aux, outs):
    outs[0][:, c0:c0 + w] = aux[0][:, c0:c0 + w] + acc


def _proj(h, w, epilogue, out_dtype, tm, tn, aux=(), aux_specs=(), chunk=512, name="proj"):
    m, d = h.shape
    n = w.shape[1]
    nj = n // tn
    chunk = min(chunk, tn)
    w_spec = _resident((d, n)) if nj == 1 else pl.BlockSpec((d, tn), lambda i, j: (0, j))
    est = (2 * _nbytes((tm, d), bf16) + (1 if nj == 1 else 2) * _nbytes((d, tn), bf16)
           + 2 * _nbytes((tm, tn), out_dtype) + 4 * _nbytes((tm, chunk), f32)
           + sum(2 * _nbytes(s.block_shape, f32) for s in aux_specs))
    body = functools.partial(_proj_body, epilogue=epilogue, chunk=chunk, n_aux=len(aux))
    return pl.pallas_call(
        body, out_shape=SDS((m, n), out_dtype), grid=(m // tm, nj),
        in_specs=[pl.BlockSpec((tm, d), lambda i, j: (i, 0)), w_spec, *aux_specs],
        out_specs=pl.BlockSpec((tm, tn), lambda i, j: (i, j)),
        compiler_params=_params(("parallel", "arbitrary"), est), name=name)(h, w, *aux)


def _xbc_body(h_ref, w_ref, cw_ref, cb_ref, o_ref, tail_ref, halo_scr, xbuf, *, chunk, tiles_per_batch):
    tm = h_ref.shape[0]
    halo = SSD_CONV - 1

    @pl.when(lax.rem(pl.program_id(0), tiles_per_batch) == 0)
    def _():
        halo_scr[...] = jnp.zeros_like(halo_scr)

    h = h_ref[...]
    for c0 in range(0, w_ref.shape[1], chunk):
        cs = slice(c0, c0 + chunk)
        acc = jnp.dot(h, w_ref[:, cs], preferred_element_type=f32)
        last = acc[tm - SUBLANES:, :]
        tail_ref[:, cs] = last
        xbuf[0:SUBLANES, :] = halo_scr[:, cs]
        xbuf[SUBLANES:, :] = acc
        halo_scr[:, cs] = last
        conv = cb_ref[:, cs] + cw_ref[halo:halo + 1, cs] * acc
        for k in range(halo):
            conv = conv + cw_ref[k:k + 1, cs] * xbuf[SUBLANES - halo + k:SUBLANES - halo + k + tm, :]
        o_ref[:, cs] = _silu(conv).astype(o_ref.dtype)


def _xbc_proj(h, w, conv_w, conv_b, tm, tiles_per_batch):
    m, d = h.shape
    n = w.shape[1]
    chunk = 512
    nb = m // (tm * tiles_per_batch)
    est = (2 * _nbytes((tm, d), bf16) + _nbytes((d, n), bf16) + 2 * _nbytes((tm, n), bf16)
           + 8 * _nbytes((tm, chunk), f32))
    body = functools.partial(_xbc_body, chunk=chunk, tiles_per_batch=tiles_per_batch)
    return pl.pallas_call(
        body, out_shape=[SDS((m, n), bf16), SDS((nb * SUBLANES, n), f32)], grid=(m // tm,),
        in_specs=[pl.BlockSpec((tm, d), lambda i: (i, 0)), _resident((d, n)), _resident(conv_w.shape),
                  _resident(conv_b.shape)],
        out_specs=[pl.BlockSpec((tm, n), lambda i: (i, 0)),
                   pl.BlockSpec((SUBLANES, n), lambda i: (i // tiles_per_batch, 0))],
        scratch_shapes=[pltpu.VMEM((SUBLANES, n), f32), pltpu.VMEM((SUBLANES + tm, chunk), f32)],
        compiler_params=_params(("arbitrary",), est), name="xbc_proj")(h, w, conv_w, conv_b)


def _swa_prompt_body(sink_ref, q_ref, k_ref, v_ref, o_ref, bias_scr, *, nblk):
    blk = SWA_BLOCK

    @pl.when(pl.program_id(0) == 0)
    def _():
        row = lax.broadcasted_iota(jnp.int32, (blk, 2 * blk), 0)
        col = lax.broadcasted_iota(jnp.int32, (blk, 2 * blk), 1)
        dist = row + blk - col
        allowed = (dist >= 0) & (dist <= WINDOW)
        distf = dist.astype(f32)
        for hh in range(SWA_HEADS):
            slope = 2.0 ** (-8.0 * (hh + 1) / SWA_HEADS) * LOG2_E
            bias = jnp.where(allowed, -slope * distf, MASKED)
            bias_scr[1, hh] = bias
            bias_scr[0, hh] = jnp.where(col >= blk, bias, MASKED)

    lane = lax.broadcasted_iota(jnp.int32, (blk, LANES), 1)
    low_half = lane < SWA_HEAD_DIM
    zero = jnp.zeros((blk, LANES), bf16)

    def block(j, carry):
        r0 = pl.multiple_of(j * blk, blk)
        rp = pl.multiple_of(jnp.maximum(j - 1, 0) * blk, blk)
        first = jnp.minimum(j, 1)
        for g in range(SWA_KV_HEADS):
            ks = slice(g * LANES, (g + 1) * LANES)
            vs = slice(2 * g * LANES, 2 * (g + 1) * LANES)
            kcat = jnp.concatenate([k_ref[pl.ds(rp, blk), ks], k_ref[pl.ds(r0, blk), ks]], axis=0)
            vcat = jnp.concatenate([v_ref[pl.ds(rp, blk), vs], v_ref[pl.ds(r0, blk), vs]], axis=0)
            for pr in range(SWA_GROUP // 2):
                c0 = (g * SWA_GROUP + 2 * pr) * SWA_HEAD_DIM
                q2 = q_ref[pl.ds(r0, blk), c0:c0 + LANES]
                res = []
                for half in range(2):
                    r = 2 * pr + half
                    qm = jnp.where(low_half, q2, zero) if half == 0 else jnp.where(low_half, zero, q2)
                    s = lax.dot_general(qm, kcat, NT_DIMS, preferred_element_type=f32)
                    s = s + bias_scr[first, g * SWA_GROUP + r]
                    sink = sink_ref[g * SWA_GROUP + r]
                    mx = jnp.maximum(jnp.max(s, axis=-1, keepdims=True), sink)
                    p = jnp.exp2(s - mx)
                    o = jnp.dot(p.astype(bf16), vcat, preferred_element_type=f32)
                    es = jnp.exp2(sink - mx)
                    lo, hi = o[:, :LANES], o[:, LANES:]
                    res.append(lo / (hi + es) if half == 0 else hi / (lo + es))
                o_ref[pl.ds(r0, blk), c0:c0 + LANES] = jnp.where(low_half, res[0], res[1]).astype(o_ref.dtype)
        return carry

    lax.fori_loop(0, nblk, block, 0)


def _swa_prompt(q, kd, v4, sinks_log2, nbatch, t):
    nq, nk, nv = q.shape[1], kd.shape[1], v4.shape[1]
    bias_shape = (2, SWA_HEADS, SWA_BLOCK, 2 * SWA_BLOCK)
    est = (2 * (2 * _nbytes((t, nq), bf16) + _nbytes((t, nk), bf16) + _nbytes((t, nv), bf16)) + _nbytes(bias_shape, f32)
           + 32 * _nbytes((SWA_BLOCK, 2 * SWA_BLOCK), f32))
    body = functools.partial(_swa_prompt_body, nblk=t // SWA_BLOCK)
    rows = lambda width: pl.BlockSpec((t, width), lambda b: (b, 0))
    return pl.pallas_call(
        body, out_shape=SDS(q.shape, bf16), grid=(nbatch,),
        in_specs=[pl.BlockSpec(memory_space=pltpu.SMEM), rows(nq), rows(nk), rows(nv)],
        out_specs=rows(nq), scratch_shapes=[pltpu.VMEM(bias_shape, f32)],
        compiler_params=_params(("arbitrary",), est), name="swa_prompt")(sinks_log2, q, kd, v4)


def _swa_decode_body(q_ref, kn_ref, vn_ref, ck_ref, cv_ref, slope_ref, sink_ref, o_ref, ok_ref, ov_ref, *, bb, w_buf):
    row_kv = lax.broadcasted_iota(jnp.int32, (SWA_HEADS, w_buf), 0) // SWA_GROUP
    row_kv_d = lax.broadcasted_iota(jnp.int32, (SWA_HEADS, SWA_HEAD_DIM), 0) // SWA_GROUP
    tok = lax.broadcasted_iota(jnp.int32, (SWA_HEADS, w_buf), 1)
    bias = slope_ref[:, :w_buf] * (tok - w_buf).astype(f32)
    sink = sink_ref[:, 0:1]

    def own_rows(per_kv):
        out = jnp.zeros((SWA_HEADS, SWA_HEAD_DIM), f32)
        for g in range(SWA_KV_HEADS):
            out = jnp.where(row_kv_d == g, per_kv[g:g + 1, :], out)
        return out

    for b in range(bb):
        q = q_ref[b].astype(bf16)
        kn, vn = kn_ref[b], vn_ref[b]
        s = jnp.zeros((SWA_HEADS, w_buf), f32)
        for g in range(SWA_KV_HEADS):
            sg = lax.dot_general(q, ck_ref[b, :, g, :].astype(bf16), NT_DIMS, preferred_element_type=f32)
            s = jnp.where(row_kv == g, sg, s)
        s = s + bias
        sn = jnp.sum(q.astype(f32) * own_rows(kn).astype(bf16).astype(f32), axis=-1, keepdims=True)
        mx = jnp.maximum(jnp.maximum(jnp.max(s, axis=-1, keepdims=True), sn), sink)
        p = jnp.exp(s - mx)
        pn = jnp.exp(sn - mx)
        den = jnp.sum(p, axis=-1, keepdims=True) + pn + jnp.exp(sink - mx)
        pb = p.astype(bf16)
        o = pn.astype(bf16).astype(f32) * own_rows(vn).astype(bf16).astype(f32)
        for g in range(SWA_KV_HEADS):
            og = jnp.dot(pb, cv_ref[b, :, g, :].astype(bf16), preferred_element_type=f32)
            o = o + jnp.where(row_kv_d == g, og, 0.0)
        o_ref[b] = (o / den).astype(o_ref.dtype)
        ok_ref[b, 0:w_buf - 1] = ck_ref[b, 1:w_buf]
        ok_ref[b, w_buf - 1] = kn
        ov_ref[b, 0:w_buf - 1] = cv_ref[b, 1:w_buf]
        ov_ref[b, w_buf - 1] = vn


def _swa_decode(q, k_new, v_new, cache_k, cache_v, slopes, sinks, bb):
    nb, w_buf, kvh, hd = cache_k.shape
    padded = _nbytes((bb, w_buf, SUBLANES, LANES), f32)
    est = 2 * 4 * padded + (2 << 20)
    body = functools.partial(_swa_decode_body, bb=bb, w_buf=w_buf)
    cache_spec = pl.BlockSpec((bb, w_buf, kvh, hd), lambda i: (i, 0, 0, 0))
    new_spec = pl.BlockSpec((bb, kvh, hd), lambda i: (i, 0, 0))
    q_spec = pl.BlockSpec((bb, SWA_HEADS, hd), lambda i: (i, 0, 0))
    return pl.pallas_call(
        body, out_shape=[SDS((nb, SWA_HEADS, hd), bf16), SDS(cache_k.shape, cache_k.dtype), SDS(cache_v.shape, cache_v.dtype)],
        grid=(nb // bb,),
        in_specs=[q_spec, new_spec, new_spec, cache_spec, cache_spec,
                  _resident((SWA_HEADS, LANES)), _resident((SWA_HEADS, LANES))],
        out_specs=[q_spec, cache_spec, cache_spec],
        compiler_params=_params(("parallel",), est), name="swa_decode")(q, k_new, v_new, cache_k, cache_v, slopes, sinks)


def _mem_attn_body(q_ref, k_ref, v_ref, o_ref, *, bb):
    heads_axis = len(k_ref.shape) == 4
    for b in range(bb):
        for h in range(MEM_HEADS):
            hs = slice(h * MEM_HEAD_DIM, (h + 1) * MEM_HEAD_DIM)
            q = q_ref[b, :, hs]
            k = (k_ref[b, :, h, :] if heads_axis else k_ref[b, :, hs]).astype(bf16)
            v = (v_ref[b, :, h, :] if heads_axis else v_ref[b, :, hs]).astype(bf16)
            s = lax.dot_general(q, k, NT_DIMS, preferred_element_type=f32)
            p = jnp.exp(s - jnp.max(s, axis=-1, keepdims=True))
            den = jnp.sum(p, axis=-1, keepdims=True)
            o = jnp.dot(p.astype(bf16), v, preferred_element_type=f32)
            o_ref[b, :, hs] = (o / den).astype(o_ref.dtype)


def _mem_attn(q, k, v, bb, tq):
    nb, t, md = q.shape
    mt = k.shape[1]
    if k.ndim == 4:
        kv_spec = pl.BlockSpec((bb, mt) + k.shape[2:], lambda i, j: (i, 0, 0, 0))
        kv_bytes = _nbytes((bb, mt, SUBLANES, k.shape[3]), f32)
    else:
        kv_spec = pl.BlockSpec((bb, mt, md), lambda i, j: (i, 0, 0))
        kv_bytes = _nbytes((bb, mt, md), f32)
    est = 2 * (2 * _nbytes((bb, tq, md), bf16) + 2 * kv_bytes) + 4 * _nbytes((tq, mt), f32) + (2 << 20)
    body = functools.partial(_mem_attn_body, bb=bb)
    return pl.pallas_call(
        body, out_shape=SDS((nb, t, md), bf16), grid=(nb // bb, t // tq),
        in_specs=[pl.BlockSpec((bb, tq, md), lambda i, j: (i, j, 0)), kv_spec, kv_spec],
        out_specs=pl.BlockSpec((bb, tq, md), lambda i, j: (i, j, 0)),
        compiler_params=_params(("parallel", "arbitrary"), est), name="mem_attn")(q, k, v)


def _ssd_gate_norm(y, z, gain):
    y = y * z
    gsz = y.shape[1] // SSD_GROUPS
    outs = []
    for g in range(SSD_GROUPS):
        yg = y[:, g * gsz:(g + 1) * gsz]
        ms = jnp.mean(yg * yg, axis=-1, keepdims=True)
        outs.append(yg * lax.rsqrt(ms + EPS) * gain[:, g * gsz:(g + 1) * gsz])
    return jnp.concatenate(outs, axis=-1)


def _ssd_prompt_body(xc_ref, z_ref, dt_ref, dtb_ref, alog_ref, dsk_ref, gn_ref, rexp_ref, ltri_ref,
                     y_ref, hout_ref, h_scr):
    c = pl.program_id(1)

    @pl.when(c == 0)
    def _():
        h_scr[...] = jnp.zeros_like(h_scr)

    for r0 in range(0, z_ref.shape[0], SSD_CHUNK):
        _ssd_chunk(xc_ref, z_ref, dt_ref, dtb_ref, alog_ref, dsk_ref, gn_ref, rexp_ref, ltri_ref, y_ref, h_scr, r0)

    @pl.when(c == pl.num_programs(1) - 1)
    def _():
        hout_ref[0] = h_scr[...].T


def _ssd_chunk(xc_ref, z_ref, dt_ref, dtb_ref, alog_ref, dsk_ref, gn_ref, rexp_ref, ltri_ref, y_ref, h_scr, r0):
    chunk = SSD_CHUNK
    di = z_ref.shape[1]
    gn_w = SSD_GROUPS * SSD_D_STATE
    gw = di // SSD_GROUPS
    rs = slice(r0, r0 + chunk)

    xs = xc_ref[rs, :di].astype(f32)
    bm = xc_ref[rs, di:di + gn_w]
    cm = xc_ref[rs, di + gn_w:]

    dt = _softplus(dt_ref[rs, :] + dtb_ref[...])
    da = dt * (-jnp.exp(alog_ref[...]))
    acs = jnp.dot(ltri_ref[...], da, precision=lax.Precision.HIGHEST, preferred_element_type=f32)
    acs_t = acs.T
    dt_t = dt.T
    rexp = rexp_ref[...]
    eacs_e = _split_dot(jnp.exp(acs), rexp)
    xw = (xs * _split_dot(jnp.exp(acs[chunk - 1:chunk, :] - acs) * dt, rexp)).astype(bf16)

    row = lax.broadcasted_iota(jnp.int32, (chunk, chunk), 0)
    col = lax.broadcasted_iota(jnp.int32, (chunk, chunk), 1)
    causal = row >= col
    low_half = col < SSD_HEAD_DIM
    zero = jnp.zeros((chunk, LANES), bf16)
    heads_per_group = SSD_HEADS // SSD_GROUPS
    ys = []
    for g in range(SSD_GROUPS):
        ns = slice(g * SSD_D_STATE, (g + 1) * SSD_D_STATE)
        ls = slice(g * gw, (g + 1) * gw)
        cb = lax.dot_general(cm[:, ns], bm[:, ns], NT_DIMS, preferred_element_type=f32)
        hg = h_scr[:, ls]
        y_off = jnp.dot(cm[:, ns], hg.astype(bf16), preferred_element_type=f32) * eacs_e[:, ls]
        y_diag = []
        for pr in range(heads_per_group // 2):
            h0 = g * heads_per_group + 2 * pr
            xp = xc_ref[rs, h0 * SSD_HEAD_DIM:h0 * SSD_HEAD_DIM + LANES]
            ws = []
            for hh in (h0, h0 + 1):
                diff = acs[:, hh:hh + 1] - acs_t[hh:hh + 1, :]
                ws.append((cb * jnp.exp(jnp.where(causal, diff, MASKED)) * dt_t[hh:hh + 1, :]).astype(bf16))
            x2 = jnp.concatenate([jnp.where(low_half, xp, zero), jnp.where(low_half, zero, xp)], axis=0)
            y_diag.append(jnp.dot(jnp.concatenate(ws, axis=1), x2, preferred_element_type=f32))
        st = lax.dot_general(bm[:, ns], xw[:, ls], TN_DIMS, preferred_element_type=f32)
        h_scr[:, ls] = eacs_e[chunk - 1:chunk, ls] * hg + st
        ys.append(jnp.concatenate(y_diag, axis=-1) + y_off)
    y = jnp.concatenate(ys, axis=-1) + dsk_ref[...] * xs
    y_ref[rs, :] = _ssd_gate_norm(y, z_ref[rs, :].astype(f32), gn_ref[...]).astype(y_ref.dtype)


def _ssd_prompt(xc, z, dt, ssd_w, nbatch, t, chunks_per_step):
    _, _, dtb, alog, dsk, gn, rexp, ltri = ssd_w
    ch, di = xc.shape[1], z.shape[1]
    tr = chunks_per_step * SSD_CHUNK
    nstep = t // tr
    rows = lambda b, c: (b * nstep + c, 0)
    est = (2 * (_nbytes((tr, ch), bf16) + 2 * _nbytes((tr, di), bf16) + _nbytes((di, SSD_D_STATE), f32))
           + _nbytes((SSD_D_STATE, di), f32) + 24 * _nbytes((SSD_CHUNK, ch), f32))
    return pl.pallas_call(
        _ssd_prompt_body,
        out_shape=[SDS((nbatch * t, di), bf16), SDS((nbatch, di, SSD_D_STATE), f32)],
        grid=(nbatch, nstep),
        in_specs=[pl.BlockSpec((tr, ch), rows), pl.BlockSpec((tr, di), rows),
                  pl.BlockSpec((tr, LANES), rows),
                  _resident(dtb.shape), _resident(alog.shape),
                  _resident(dsk.shape), _resident(gn.shape), _resident(rexp.shape), _resident(ltri.shape)],
        out_specs=[pl.BlockSpec((tr, di), rows), pl.BlockSpec((1, di, SSD_D_STATE), lambda b, c: (b, 0, 0))],
        scratch_shapes=[pltpu.VMEM((SSD_D_STATE, di), f32)],
        compiler_params=_params(("parallel", "arbitrary"), est), name="ssd_prompt")(xc, z, dt, dtb, alog, dsk, gn, rexp, ltri)


def _ssd_step_body(xbc_ref, z_ref, dt_ref, cs_ref, h_ref, cw_ref, cb_ref, dtb_ref, alog_ref, dsk_ref, gn_ref, rexp_ref,
                   y_ref, cso_ref, ho_ref, xdt_scr, da_scr, bm_scr, cm_scr, y_scr, *, bb):
    ch = xbc_ref.shape[1]
    di = z_ref.shape[1]
    gn_w = SSD_GROUPS * SSD_D_STATE
    gw = di // SSD_GROUPS
    halo = SSD_CONV - 1
    xr = xbc_ref[...]
    conv = cb_ref[...] + cw_ref[halo:halo + 1, :] * xr
    for k in range(halo):
        conv = conv + cw_ref[k:k + 1, :] * cs_ref[:, k * ch:(k + 1) * ch]
    for k in range(1, halo):
        cso_ref[:, (k - 1) * ch:k * ch] = cs_ref[:, k * ch:(k + 1) * ch]
    cso_ref[:, (halo - 1) * ch:] = xr
    xc = _silu(conv)
    xs = xc[:, :di]
    dt = _softplus(dt_ref[...] + dtb_ref[...])
    rexp = rexp_ref[...]
    xdt_scr[...] = xs * _split_dot(dt, rexp)
    da_scr[...] = _split_dot(jnp.exp(dt * (-jnp.exp(alog_ref[...]))), rexp)
    bm_scr[...] = xc[:, di:di + gn_w]
    cm_scr[...] = xc[:, di + gn_w:]

    rows = 2 * SUBLANES
    rowi = lax.broadcasted_iota(jnp.int32, (rows, di), 0)
    grp = lax.broadcasted_iota(jnp.int32, (rows, di), 1) // gw
    rown = lax.broadcasted_iota(jnp.int32, (rows, SSD_D_STATE), 0)
    ones_rows = jnp.where((rown == SSD_GROUPS) | (rown == SSD_GROUPS + 1), 1.0, 0.0).astype(f32)

    def per_row(b, carry):
        xrow = xdt_scr[pl.ds(b, 1), :]
        drow = da_scr[pl.ds(b, 1), :]
        d_hi = drow.astype(bf16).astype(f32)
        lhs = jnp.where(rowi == grp, xrow, jnp.where(rowi == SSD_GROUPS, d_hi, jnp.where(rowi == SSD_GROUPS + 1, drow - d_hi, 0.0)))
        brow = bm_scr[pl.ds(b, 1), :]
        crow = cm_scr[pl.ds(b, 1), :]
        rhs_b = jnp.zeros((rows, SSD_D_STATE), f32)
        c_rows = jnp.zeros((rows, SSD_D_STATE), f32)
        for g in range(SSD_GROUPS):
            ns = slice(g * SSD_D_STATE, (g + 1) * SSD_D_STATE)
            rhs_b = jnp.where(rown == g, brow[:, ns], rhs_b)
            c_rows = jnp.where(rown == g, crow[:, ns], c_rows)
        rhs = jnp.concatenate([rhs_b, ones_rows], axis=1).astype(bf16)
        sd = lax.dot_general(lhs.astype(bf16), rhs, TN_DIMS, preferred_element_type=f32)
        hn = sd[:, SSD_D_STATE:] * h_ref[b] + sd[:, :SSD_D_STATE]
        ho_ref[b] = hn
        y8 = lax.dot_general(c_rows.astype(bf16), hn.astype(bf16), NT_DIMS, preferred_element_type=f32)
        y_scr[pl.ds(b, 1), :] = jnp.sum(jnp.where(rowi == grp, y8, 0.0), axis=0, keepdims=True)
        return carry

    lax.fori_loop(0, bb, per_row, 0)
    y = y_scr[...] + dsk_ref[...] * xs
    y_ref[...] = _ssd_gate_norm(y, _silu(z_ref[...]), gn_ref[...]).astype(y_ref.dtype)


def _ssd_step(xbc, z, dt, conv_state, h0, ssd_w, bb):
    cw, cb, dtb, alog, dsk, gn, rexp, _ = ssd_w
    nb, ch = xbc.shape
    di = z.shape[1]
    gn_w = SSD_GROUPS * SSD_D_STATE
    halo = SSD_CONV - 1
    est = (2 * (2 * _nbytes((bb, di, SSD_D_STATE), f32) + 2 * _nbytes((bb, halo * ch), f32) + 4 * _nbytes((bb, ch), f32))
           + 8 * _nbytes((di, 2 * SSD_D_STATE), f32))
    body = functools.partial(_ssd_step_body, bb=bb)
    r2 = lambda i: (i, 0)
    return pl.pallas_call(
        body,
        out_shape=[SDS((nb, di), bf16), SDS((nb, halo * ch), f32), SDS((nb, di, SSD_D_STATE), f32)],
        grid=(nb // bb,),
        in_specs=[pl.BlockSpec((bb, ch), r2), pl.BlockSpec((bb, di), r2), pl.BlockSpec((bb, LANES), r2),
                  pl.BlockSpec((bb, halo * ch), r2), pl.BlockSpec((bb, di, SSD_D_STATE), lambda i: (i, 0, 0)),
                  _resident(cw.shape), _resident(cb.shape), _resident(dtb.shape), _resident(alog.shape),
                  _resident(dsk.shape), _resident(gn.shape), _resident(rexp.shape)],
        out_specs=[pl.BlockSpec((bb, di), r2), pl.BlockSpec((bb, halo * ch), r2),
                   pl.BlockSpec((bb, di, SSD_D_STATE), lambda i: (i, 0, 0))],
        scratch_shapes=[pltpu.VMEM((bb, di), f32), pltpu.VMEM((bb, di), f32), pltpu.VMEM((bb, gn_w), f32),
                        pltpu.VMEM((bb, gn_w), f32), pltpu.VMEM((bb, di), f32)],
        compiler_params=_params(("parallel",), est), name="ssd_step")(xbc, z, dt, conv_state, h0, cw, cb, dtb, alog, dsk, gn, rexp)


def _merge_body(a_ref, s_ref, m_ref, g0_ref, g1_ref, g2_ref, wa_ref, ws_ref, wm_ref, o_ref, *, chunk):
    a, s, m = a_ref[...], s_ref[...], m_ref[...]
    for c0 in range(0, o_ref.shape[1], chunk):
        cs = slice(c0, c0 + chunk)
        acc = g0_ref[:, cs].astype(f32) * jnp.dot(a, wa_ref[:, cs], preferred_element_type=f32)
        acc = acc + g1_ref[:, cs].astype(f32) * jnp.dot(s, ws_ref[:, cs], preferred_element_type=f32)
        acc = acc + g2_ref[:, cs].astype(f32) * jnp.dot(m, wm_ref[:, cs], preferred_element_type=f32)
        o_ref[:, cs] = acc.astype(o_ref.dtype)


def _merge(a, s, mo, gates, wa, ws, wm, tm):
    m, d = s.shape[0], wa.shape[1]
    chunk = 512
    est = (2 * (_nbytes((tm, a.shape[1]), bf16) + _nbytes((tm, s.shape[1]), bf16) + _nbytes((tm, mo.shape[1]), bf16)
                + 4 * _nbytes((tm, d), bf16))
           + _nbytes(wa.shape, bf16) + _nbytes(ws.shape, bf16) + _nbytes(wm.shape, bf16) + 6 * _nbytes((tm, chunk), f32))
    body = functools.partial(_merge_body, chunk=chunk)
    return pl.pallas_call(
        body, out_shape=SDS((m, d), bf16), grid=(m // tm,),
        in_specs=[pl.BlockSpec((tm, a.shape[1]), lambda i: (i, 0)), pl.BlockSpec((tm, s.shape[1]), lambda i: (i, 0)),
                  pl.BlockSpec((tm, mo.shape[1]), lambda i: (i, 0)),
                  pl.BlockSpec((tm, d), lambda i: (i, 0)), pl.BlockSpec((tm, d), lambda i: (i, 1)),
                  pl.BlockSpec((tm, d), lambda i: (i, 2)),
                  _resident(wa.shape), _resident(ws.shape), _resident(wm.shape)],
        out_specs=pl.BlockSpec((tm, d), lambda i: (i, 0)),
        compiler_params=_params(("parallel",), est), name="merge")(a, s, mo, gates, gates, gates, wa, ws, wm)


def _ffn_body(x_ref, g_ref, wg_ref, wu_ref, wd_ref, o_ref, h_scr):
    @pl.when(pl.program_id(1) == 0)
    def _():
        x = x_ref[...]
        ms = jnp.mean(x * x, axis=-1, keepdims=True)
        h_scr[...] = (x * lax.rsqrt(ms + EPS) * g_ref[...]).astype(bf16)
        o_ref[...] = x

    h = h_scr[...]
    gate = jnp.dot(h, wg_ref[...], preferred_element_type=f32)
    up = jnp.dot(h, wu_ref[...], preferred_element_type=f32)
    act = (_silu(gate) * up).astype(bf16)
    chunk = 512
    for n0 in range(0, o_ref.shape[1], chunk):
        o_ref[:, n0:n0 + chunk] += jnp.dot(act, wd_ref[:, n0:n0 + chunk], preferred_element_type=f32)


def _ffn(x, gain, wg, wu, wd, tm, tf):
    m, d = x.shape
    dff = wg.shape[1]
    est = (4 * _nbytes((tm, d), f32) + _nbytes((tm, d), bf16) + 2 * 3 * _nbytes((d, tf), bf16)
           + 4 * _nbytes((tm, tf), f32))
    return pl.pallas_call(
        _ffn_body, out_shape=SDS((m, d), f32), grid=(m // tm, dff // tf),
        in_specs=[pl.BlockSpec((tm, d), lambda i, c: (i, 0)), _resident((1, d)),
                  pl.BlockSpec((d, tf), lambda i, c: (0, c)), pl.BlockSpec((d, tf), lambda i, c: (0, c)),
                  pl.BlockSpec((tf, d), lambda i, c: (c, 0))],
        out_specs=pl.BlockSpec((tm, d), lambda i, c: (i, 0)),
        scratch_shapes=[pltpu.VMEM((tm, d), bf16)],
        compiler_params=_params(("parallel", "arbitrary"), est), name="ffn")(x, gain.reshape(1, d), wg, wu, wd)


def _dup_heads(w):
    lead = w.shape[:-1]
    w = w.reshape(*lead, SWA_KV_HEADS, 1, SWA_HEAD_DIM)
    return jnp.broadcast_to(w, (*lead, SWA_KV_HEADS, 2, SWA_HEAD_DIM)).reshape(*lead, SWA_KV_HEADS * LANES)


def _pad_lanes(v, n=LANES):
    return jnp.pad(v, [(0, 0)] * (v.ndim - 1) + [(0, n - v.shape[-1])])


def _layer(xp, xs, cache_k, cache_v, cmem_k, cmem_v, state_ssm, state_conv, mem_prompt, w):
    bp, tp, d = xp.shape
    bs = xs.shape[0]
    w_buf = cache_k.shape[1]
    assert xs.shape[1] == 1 and w_buf == WINDOW and tp % SSD_CHUNK == 0 and bs % SUBLANES == 0
    q_dim = SWA_HEADS * SWA_HEAD_DIM
    kv_dim = SWA_KV_HEADS * SWA_HEAD_DIM
    di = SSD_HEADS * SSD_HEAD_DIM
    ch = di + 2 * SSD_GROUPS * SSD_D_STATE
    md = MEM_HEADS * MEM_HEAD_DIM
    o = 0
    w_in = w["w_in"].astype(bf16)
    wq, o = w_in[:, o:o + q_dim], o + q_dim
    wk, o = w_in[:, o:o + kv_dim], o + kv_dim
    wv, o = w_in[:, o:o + kv_dim], o + kv_dim
    wz, o = w_in[:, o:o + di], o + di
    wxbc, o = w_in[:, o:o + ch], o + ch
    wdt, o = w_in[:, o:o + SSD_HEADS], o + SSD_HEADS
    wqm, o = w_in[:, o:o + md], o + md
    wg = w_in[:, o:]
    wdt = _pad_lanes(wdt)

    q_scale = SWA_HEAD_DIM ** -0.5
    gq = jnp.tile(w["q_norm_swa"], SWA_HEADS) * q_scale
    gk = jnp.tile(w["k_norm_swa"], SWA_KV_HEADS)
    head_of = jnp.arange(2 * LANES) // SWA_HEAD_DIM
    blockdiag = (head_of[:, None] == head_of[None, :]).astype(bf16)
    w_qkv_p = jnp.concatenate([wq, _dup_heads(wk), _dup_heads(wv), wdt], axis=1)
    w_qkv_s = jnp.concatenate([wq, wk, wv, wdt], axis=1)
    gqm = (jnp.tile(w["q_norm_mem"], MEM_HEADS) * MEM_HEAD_DIM ** -0.5).reshape(1, md)
    gkm = jnp.tile(w["k_norm_mem"], MEM_HEADS).reshape(1, md)
    wz_b, wxbc_b, wqm_b, wg_b = wz, wxbc, wqm, wg
    w_mem_k, w_mem_v = w["w_mem_kv"][:, :md].astype(bf16), w["w_mem_kv"][:, md:].astype(bf16)
    wa, ws, wm = w["w_up_swa"].astype(bf16), w["w_up_ssd"].astype(bf16), w["w_up_mem"].astype(bf16)
    w_out = w["w_out"].astype(bf16)
    w_gate, w_up, w_down = w["w_gate"].astype(bf16), w["w_up"].astype(bf16), w["w_down"].astype(bf16)

    head_of_lane = jnp.arange(di) // SSD_HEAD_DIM
    rexp = (jnp.arange(LANES)[:, None] == head_of_lane[None, :]).astype(bf16)
    ltri = (jnp.arange(SSD_CHUNK)[:, None] >= jnp.arange(SSD_CHUNK)[None, :]).astype(f32)
    ssd_w = (w["conv_w"], w["conv_b"].reshape(1, ch), _pad_lanes(w["dt_bias"].reshape(1, -1)),
             _pad_lanes(w["a_log"].reshape(1, -1)), jnp.repeat(w["d_skip"], SSD_HEAD_DIM).reshape(1, di),
             w["ssd_norm"].reshape(1, di), rexp, ltri)
    slopes = jnp.broadcast_to(jnp.exp2(-8.0 * jnp.arange(1, SWA_HEADS + 1, dtype=f32) / SWA_HEADS)[:, None], (SWA_HEADS, LANES))
    sinks_b = jnp.broadcast_to(w["swa_sinks"].astype(f32)[:, None], (SWA_HEADS, LANES))

    def finish(x2, a_out, s_out, m_out, gates, tm, tm_ffn):
        merged = _merge(a_out, s_out, m_out, gates, wa, ws, wm, tm)
        x1 = _proj(merged, w_out, _epi_residual, f32, tm, d, aux=(x2,), aux_specs=(pl.BlockSpec((tm, d), lambda i, j: (i, 0)),),
                   name="out_proj")
        return _ffn(x1, w["norm_ffn"], w_gate, w_up, w_down, tm_ffn, 512)

    mp = bp * tp
    x2 = xp.reshape(mp, d)
    tm = min(1024, tp)
    tpb = tp // tm
    tm_qkv = min(512, tp)
    g_mix = w["norm_mix"].reshape(1, d)
    h, q, kd, v4, dt, klast, vlast = _qkv_proj(
        x2, g_mix, w_qkv_p, (gq * LOG2_E).reshape(1, -1), _dup_heads(gk).reshape(1, -1), blockdiag,
        q_dim, 2 * kv_dim, 2 * kv_dim, tm_qkv, tp // tm_qkv, bf16, bf16, True)
    z = _proj(h, wz_b, _epi_silu, bf16, tm, di, name="z_proj")
    xc, xtail = _xbc_proj(h, wxbc_b, ssd_w[0], ssd_w[1], tm, tpb)
    qm = _proj(h, wqm_b, _epi_headnorm256, bf16, tm, md, aux=(gqm,), aux_specs=(_resident((1, md)),), name="qm_proj")
    gates = _proj(h, wg_b, _epi_sigmoid, bf16, tm, d, name="gate_proj")
    a_out = _swa_prompt(q, kd, v4, w["swa_sinks"].astype(f32) * LOG2_E, bp, tp)
    s_out, p_h = _ssd_prompt(xc, z, dt, ssd_w, bp, tp, 4 if tp % (4 * SSD_CHUNK) == 0 else 1)
    mt = mem_prompt.shape[1]
    hm = _rmsnorm(mem_prompt.reshape(bp * mt, d), w["norm_mem"], mt)
    mk = _proj(hm, w_mem_k, _epi_headnorm256, f32, mt, md, aux=(gkm,), aux_specs=(_resident((1, md)),), name="mem_k_proj")
    mv = _proj(hm, w_mem_v, _epi_plain, f32, mt, md, name="mem_v_proj")
    m_out = _mem_attn(qm.reshape(bp, tp, md), mk.reshape(bp, mt, md), mv.reshape(bp, mt, md), 1, tm)
    yp = finish(x2, a_out, s_out, m_out.reshape(mp, md), gates, min(512, tp), min(512, tp)).reshape(bp, tp, d)

    undup = lambda t: t.reshape(bp, WINDOW, SWA_KV_HEADS, 2, SWA_HEAD_DIM)[:, :, :, 0, :]
    p_k, p_v = undup(klast), undup(vlast)
    p_mk = mk.reshape(bp, mt, MEM_HEADS, MEM_HEAD_DIM)
    p_mv = mv.reshape(bp, mt, MEM_HEADS, MEM_HEAD_DIM)
    p_h = p_h.reshape(bp, SSD_HEADS, SSD_HEAD_DIM, SSD_D_STATE)
    p_c = xtail.reshape(bp, SUBLANES, ch)[:, SUBLANES - (SSD_CONV - 1):, :]

    xs2 = xs.reshape(bs, d)
    hs, qs, ks, vs, dts = _qkv_proj(xs2, g_mix, w_qkv_s, gq.reshape(1, -1), gk.reshape(1, -1), blockdiag,
                                    q_dim, kv_dim, kv_dim, bs, 1, f32, f32, False)
    zs = _proj(hs, wz_b, _epi_plain, f32, bs, di, name="z_proj_s")
    xbcs = _proj(hs, wxbc_b, _epi_plain, f32, bs, ch, name="xbc_proj_s")
    qms = _proj(hs, wqm_b, _epi_headnorm256, bf16, bs, md, aux=(gqm,), aux_specs=(_resident((1, md)),), name="qm_proj_s")
    gates_s = _proj(hs, wg_b, _epi_sigmoid, bf16, bs, d, name="gate_proj_s")
    a_s, s_k, s_v = _swa_decode(qs.reshape(bs, SWA_HEADS, SWA_HEAD_DIM), ks.reshape(bs, SWA_KV_HEADS, SWA_HEAD_DIM),
                                vs.reshape(bs, SWA_KV_HEADS, SWA_HEAD_DIM), cache_k, cache_v, slopes, sinks_b, SUBLANES)
    halo = SSD_CONV - 1
    s_s, s_c, s_h = _ssd_step(xbcs, zs, dts, state_conv.reshape(bs, halo * ch), state_ssm.reshape(bs, di, SSD_D_STATE),
                              ssd_w, SUBLANES)
    qm8 = jnp.broadcast_to(qms[:, None, :], (bs, 2 * SUBLANES, md))
    m_s = _mem_attn(qm8, cmem_k, cmem_v, 2, 2 * SUBLANES)[:, 0, :]
    ys = finish(xs2, a_s.reshape(bs, q_dim), s_s, m_s, gates_s, bs, bs).reshape(bs, 1, d)

    s_h =s_h.reshape(bs, SSD_HEADS, SSD_HEAD_DIM, SSD_D_STATE)
    s_c = s_c.reshape(bs, halo, ch)
    return yp, ys, (p_k, p_v, p_mk, p_mv, p_h, p_c), (s_k, s_v, s_h, s_c)


_WEIGHT_NAMES = ("norm_mix", "w_in", "q_norm_swa", "k_norm_swa", "swa_sinks", "conv_w", "conv_b", "dt_bias", "a_log",
                 "d_skip", "ssd_norm", "norm_mem", "w_mem_kv", "q_norm_mem", "k_norm_mem", "w_up_swa", "w_up_ssd",
                 "w_up_mem", "w_out", "norm_ffn", "w_gate", "w_up", "w_down")


def kernel(x_prompt, x_sample, cache_swa_k, cache_swa_v, cache_mem_k, cache_mem_v, state_ssm, state_conv, mem_prompt, norm_mix, w_in, q_norm_swa, k_norm_swa, swa_sinks, conv_w, conv_b, dt_bias, a_log, d_skip, ssd_norm, norm_mem, w_mem_kv, q_norm_mem, k_norm_mem, w_up_swa, w_up_ssd, w_up_mem, w_out, norm_ffn, w_gate, w_up, w_down):
    weights = (norm_mix, w_in, q_norm_swa, k_norm_swa, swa_sinks, conv_w, conv_b, dt_bias, a_log, d_skip, ssd_norm,
               norm_mem, w_mem_kv, q_norm_mem, k_norm_mem, w_up_swa, w_up_ssd, w_up_mem, w_out, norm_ffn, w_gate, w_up, w_down)
    depth = w_in.shape[0]
    layer = (lambda a, l: a.reshape(a.shape[1:])) if depth == 1 else (lambda a, l: a[l])
    yp, ys = x_prompt, x_sample
    p_outs, s_outs = [], []
    for l in range(depth):
        w = {n: layer(a, l) for n, a in zip(_WEIGHT_NAMES, weights)}
        yp, ys, po, so = _layer(yp, ys, layer(cache_swa_k, l), layer(cache_swa_v, l), layer(cache_mem_k, l),
                                layer(cache_mem_v, l), layer(state_ssm, l), layer(state_conv, l), mem_prompt, w)
        p_outs.append(po)
        s_outs.append(so)
    stack = lambda outs, i: jnp.stack([o[i] for o in outs])
    return (yp, ys, *(stack(p_outs, i) for i in range(6)), *(stack(s_outs, i) for i in range(4)))
```

```python
import functools

import jax
import jax.numpy as jnp
from jax import lax
from jax.experimental import pallas as pl
from jax.experimental.pallas import tpu as pltpu

f32 = jnp.float32
bf16 = jnp.bfloat16
SDS = jax.ShapeDtypeStruct

LANES = 128
SUBLANES = 8
VMEM_BYTES_V7X = 64 * 1024 * 1024
VMEM_HEADROOM = 8 * 1024 * 1024

EPS = 1e-6
SWA_HEADS = 16
SWA_KV_HEADS = 4
SWA_GROUP = SWA_HEADS // SWA_KV_HEADS
SWA_HEAD_DIM = 64
WINDOW = 128
SWA_BLOCK = 128
SSD_HEAD_DIM = 64
SSD_HEADS = 32
SSD_GROUPS = 4
SSD_D_STATE = 128
SSD_CONV = 4
SSD_CHUNK = 128
MEM_HEADS = 4
MEM_HEAD_DIM = 256
N_BRANCH = 3
MASKED = -1e30
LOG2_E = 1.4426950408889634

NT_DIMS = (((1,), (1,)), ((), ()))
TN_DIMS = (((0,), (0,)), ((), ()))


def _params(semantics, block_bytes):
    limit = min(int(block_bytes) + VMEM_HEADROOM, VMEM_BYTES_V7X - VMEM_HEADROOM)
    return pltpu.CompilerParams(dimension_semantics=semantics, vmem_limit_bytes=limit)


def _nbytes(shape, dtype):
    n = 1
    for s in shape:
        n *= s
    return n * jnp.dtype(dtype).itemsize


def _resident(shape):
    nd = len(shape)
    return pl.BlockSpec(shape, lambda *_: (0,) * nd, pipeline_mode=pl.Buffered(1))


def _sigmoid(x):
    return 0.5 + 0.5 * jnp.tanh(0.5 * x)


def _silu(x):
    hx = 0.5 * x
    return hx + hx * jnp.tanh(hx)


def _softplus(x):
    return jnp.maximum(x, 0.0) + jnp.log1p(jnp.exp(-jnp.abs(x)))


def _split_dot(v, m):
    hi = v.astype(bf16)
    lo = (v - hi.astype(f32)).astype(bf16)
    return jnp.dot(hi, m, preferred_element_type=f32) + jnp.dot(lo, m, preferred_element_type=f32)


def _headnorm64(acc, gain, blockdiag):
    outs = []
    wide = blockdiag.shape[0]
    for c in range(acc.shape[1] // wide):
        a = acc[:, c * wide:(c + 1) * wide]
        ss = jnp.dot((a * a).astype(bf16), blockdiag, preferred_element_type=f32)
        outs.append(a * lax.rsqrt(ss * (1.0 / SWA_HEAD_DIM) + EPS) * gain[:, c * wide:(c + 1) * wide])
    return outs[0] if len(outs) == 1 else jnp.concatenate(outs, axis=-1)


def _norm_body(x_ref, g_ref, o_ref):
    x = x_ref[...]
    ms = jnp.mean(x * x, axis=-1, keepdims=True)
    o_ref[...] = (x * lax.rsqrt(ms + EPS) * g_ref[...]).astype(o_ref.dtype)


def _rmsnorm(x, gain, tm):
    m, d = x.shape
    est = 2 * (_nbytes((tm, d), f32) + _nbytes((tm, d), bf16)) + _nbytes((tm, d), f32)
    return pl.pallas_call(
        _norm_body, out_shape=SDS((m, d), bf16), grid=(m // tm,),
        in_specs=[pl.BlockSpec((tm, d), lambda i: (i, 0)), _resident((1, d))],
        out_specs=pl.BlockSpec((tm, d), lambda i: (i, 0)),
        compiler_params=_params(("parallel",), est), name="rmsnorm")(x, gain.reshape(1, d))


def _qkv_body(x_ref, gx_ref, w_ref, gq_ref, gk_ref, bd_ref, h_ref, q_ref, k_ref, *rest, nq, nk, nv, chunk, prompt):
    v_ref, dt_ref, *last_refs = rest
    x = x_ref[...]
    h = (x * lax.rsqrt(jnp.mean(x * x, axis=-1, keepdims=True) + EPS) * gx_ref[...]).astype(bf16)
    h_ref[...] = h
    bd = bd_ref[...]
    tm = h.shape[0]
    for c0 in range(0, nq, chunk):
        acc = jnp.dot(h, w_ref[:, c0:c0 + chunk], preferred_element_type=f32)
        q_ref[:, c0:c0 + chunk] = _headnorm64(acc, gq_ref[:, c0:c0 + chunk], bd).astype(q_ref.dtype)
    for c0 in range(0, nk, chunk):
        w = min(chunk, nk - c0)
        acc = jnp.dot(h, w_ref[:, nq + c0:nq + c0 + w], preferred_element_type=f32)
        kn = _headnorm64(acc, gk_ref[:, c0:c0 + w], bd)
        k_ref[:, c0:c0 + w] = kn.astype(k_ref.dtype)
        if last_refs:
            last_refs[0][:, c0:c0 + w] = kn[tm - WINDOW:, :]
    for c0 in range(0, nv, chunk):
        w = min(chunk, nv - c0)
        acc = jnp.dot(h, w_ref[:, nq + nk + c0:nq + nk + c0 + w], preferred_element_type=f32)
        if prompt:
            low_half = lax.broadcasted_iota(jnp.int32, (tm, LANES), 1) < SWA_HEAD_DIM
            for t0 in range(0, w, LANES):
                a = acc[:, t0:t0 + LANES]
                v_ref[:, 2 * (c0 + t0):2 * (c0 + t0) + LANES] = jnp.where(low_half, a, 1.0).astype(v_ref.dtype)
                v_ref[:, 2 * (c0 + t0) + LANES:2 * (c0 + t0 + LANES)] = jnp.where(low_half, 1.0, a).astype(v_ref.dtype)
            last_refs[1][:, c0:c0 + w] = acc[tm - WINDOW:, :]
        else:
            v_ref[:, c0:c0 + w] = acc.astype(v_ref.dtype)
    dt_ref[...] = jnp.dot(h, w_ref[:, nq + nk + nv:], preferred_element_type=f32)


def _qkv_proj(x, gx, w, gq, gk, blockdiag, nq, nk, nv, tm, tiles_per_batch, q_dtype, kv_dtype, prompt):
    m, d = x.shape
    n = w.shape[1]
    chunk = 512
    rows = lambda width: pl.BlockSpec((tm, width), lambda i: (i, 0))
    v_width = 2 * nv if prompt else nv
    out_shape = [SDS((m, d), bf16), SDS((m, nq), q_dtype), SDS((m, nk), kv_dtype), SDS((m, v_width), kv_dtype),
                 SDS((m, LANES), f32)]
    out_specs = [rows(d), rows(nq), rows(nk), rows(v_width), rows(LANES)]
    if prompt:
        nb = m // (tm * tiles_per_batch)
        out_shape += [SDS((nb * WINDOW, nk), f32), SDS((nb * WINDOW, nv), f32)]
        out_specs += [pl.BlockSpec((WINDOW, nk), lambda i: (i // tiles_per_batch, 0)),
                      pl.BlockSpec((WINDOW, nv), lambda i: (i // tiles_per_batch, 0))]
    est = (2 * _nbytes((tm, d), f32) + 3 * _nbytes((tm, d), bf16) + _nbytes((d, n), bf16)
           + 2 * _nbytes((tm, nq + nk + 2 * nv), f32) + 4 * _nbytes((tm, chunk), f32))
    body = functools.partial(_qkv_body, nq=nq, nk=nk, nv=nv, chunk=chunk, prompt=prompt)
    return pl.pallas_call(
        body, out_shape=out_shape, grid=(m // tm,),
        in_specs=[pl.BlockSpec((tm, d), lambda i: (i, 0)), _resident((1, d)), _resident((d, n)), _resident((1, nq)),
                  _resident((1, nk)), _resident(blockdiag.shape)],
        out_specs=out_specs, compiler_params=_params(("arbitrary",), est), name="qkv_proj")(x, gx, w, gq, gk, blockdiag)


def _proj_body(h_ref, w_ref, *refs, epilogue, chunk, n_aux):
    aux, outs = refs[:n_aux], refs[n_aux:]
    h = h_ref[...]
    tn = w_ref.shape[1]
    for c0 in range(0, tn, chunk):
        acc = jnp.dot(h, w_ref[:, c0:c0 + chunk], preferred_element_type=f32)
        epilogue(acc, c0, chunk, aux, outs)


def _epi_plain(acc, c0, w, aux, outs):
    outs[0][:, c0:c0 + w] = acc.astype(outs[0].dtype)


def _epi_silu(acc, c0, w, aux, outs):
    outs[0][:, c0:c0 + w] = _silu(acc).astype(outs[0].dtype)


def _epi_sigmoid(acc, c0, w, aux, outs):
    outs[0][:, c0:c0 + w] = _sigmoid(acc).astype(outs[0].dtype)


def _epi_headnorm256(acc, c0, w, aux, outs):
    gain = aux[0]
    for h0 in range(0, w, MEM_HEAD_DIM):
        a = acc[:, h0:h0 + MEM_HEAD_DIM]
        ms = jnp.mean(a * a, axis=-1, keepdims=True)
        y = a * lax.rsqrt(ms + EPS) * gain[:, c0 + h0:c0 + h0 + MEM_HEAD_DIM]
        outs[0][:, c0 + h0:c0 + h0 + MEM_HEAD_DIM] = y.astype(outs[0].dtype)


def _epi_residual(acc, c0, w, aux, outs):
    outs[0][:, c0:c0 + w] = aux[0][:, c0:c0 + w] + acc


def _proj(h, w, epilogue, out_dtype, tm, tn, aux=(), aux_specs=(), chunk=512, name="proj"):
    m, d = h.shape
    n = w.shape[1]
    nj = n // tn
    chunk = min(chunk, tn)
    w_spec = _resident((d, n)) if nj == 1 else pl.BlockSpec((d, tn), lambda i, j: (0, j))
    est = (2 * _nbytes((tm, d), bf16) + (1 if nj == 1 else 2) * _nbytes((d, tn), bf16)
           + 2 * _nbytes((tm, tn), out_dtype) + 4 * _nbytes((tm, chunk), f32)
           + sum(2 * _nbytes(s.block_shape, f32) for s in aux_specs))
    body = functools.partial(_proj_body, epilogue=epilogue, chunk=chunk, n_aux=len(aux))
    return pl.pallas_call(
        body, out_shape=SDS((m, n), out_dtype), grid=(m // tm, nj),
        in_specs=[pl.BlockSpec((tm, d), lambda i, j: (i, 0)), w_spec, *aux_specs],
        out_specs=pl.BlockSpec((tm, tn), lambda i, j: (i, j)),
        compiler_params=_params(("parallel", "arbitrary"), est), name=name)(h, w, *aux)


def _xbc_body(h_ref, w_ref, cw_ref, cb_ref, o_ref, tail_ref, halo_scr, xbuf, *, chunk, tiles_per_batch):
    tm = h_ref.shape[0]
    halo = SSD_CONV - 1

    @pl.when(lax.rem(pl.program_id(0), tiles_per_batch) == 0)
    def _():
        halo_scr[...] = jnp.zeros_like(halo_scr)

    h = h_ref[...]
    for c0 in range(0, w_ref.shape[1], chunk):
        cs = slice(c0, c0 + chunk)
        acc = jnp.dot(h, w_ref[:, cs], preferred_element_type=f32)
        last = acc[tm - SUBLANES:, :]
        tail_ref[:, cs] = last
        xbuf[0:SUBLANES, :] = halo_scr[:, cs]
        xbuf[SUBLANES:, :] = acc
        halo_scr[:, cs] = last
        conv = cb_ref[:, cs] + cw_ref[halo:halo + 1, cs] * acc
        for k in range(halo):
            conv = conv + cw_ref[k:k + 1, cs] * xbuf[SUBLANES - halo + k:SUBLANES - halo + k + tm, :]
        o_ref[:, cs] = _silu(conv).astype(o_ref.dtype)


def _xbc_proj(h, w, conv_w, conv_b, tm, tiles_per_batch):
    m, d = h.shape
    n = w.shape[1]
    chunk = 512
    nb = m // (tm * tiles_per_batch)
    est = (2 * _nbytes((tm, d), bf16) + _nbytes((d, n), bf16) + 2 * _nbytes((tm, n), bf16)
           + 8 * _nbytes((tm, chunk), f32))
    body = functools.partial(_xbc_body, chunk=chunk, tiles_per_batch=tiles_per_batch)
    return pl.pallas_call(
        body, out_shape=[SDS((m, n), bf16), SDS((nb * SUBLANES, n), f32)], grid=(m // tm,),
        in_specs=[pl.BlockSpec((tm, d), lambda i: (i, 0)), _resident((d, n)), _resident(conv_w.shape),
                  _resident(conv_b.shape)],
        out_specs=[pl.BlockSpec((tm, n), lambda i: (i, 0)),
                   pl.BlockSpec((SUBLANES, n), lambda i: (i // tiles_per_batch, 0))],
        scratch_shapes=[pltpu.VMEM((SUBLANES, n), f32), pltpu.VMEM((SUBLANES + tm, chunk), f32)],
        compiler_params=_params(("arbitrary",), est), name="xbc_proj")(h, w, conv_w, conv_b)


def _swa_prompt_body(sink_ref, q_ref, k_ref, v_ref, o_ref, bias_scr, *, nblk):
    blk = SWA_BLOCK

    @pl.when(pl.program_id(0) == 0)
    def _():
        row = lax.broadcasted_iota(jnp.int32, (blk, 2 * blk), 0)
        col = lax.broadcasted_iota(jnp.int32, (blk, 2 * blk), 1)
        dist = row + blk - col
        allowed = (dist >= 0) & (dist <= WINDOW)
        distf = dist.astype(f32)
        for hh in range(SWA_HEADS):
            slope = 2.0 ** (-8.0 * (hh + 1) / SWA_HEADS) * LOG2_E
            bias = jnp.where(allowed, -slope * distf, MASKED)
            bias_scr[1, hh] = bias
            bias_scr[0, hh] = jnp.where(col >= blk, bias, MASKED)

    lane = lax.broadcasted_iota(jnp.int32, (blk, LANES), 1)
    low_half = lane < SWA_HEAD_DIM
    zero = jnp.zeros((blk, LANES), bf16)

    def block(j, carry):
        r0 = pl.multiple_of(j * blk, blk)
        rp = pl.multiple_of(jnp.maximum(j - 1, 0) * blk, blk)
        first = jnp.minimum(j, 1)
        for g in range(SWA_KV_HEADS):
            ks = slice(g * LANES, (g + 1) * LANES)
            vs = slice(2 * g * LANES, 2 * (g + 1) * LANES)
            kcat = jnp.concatenate([k_ref[pl.ds(rp, blk), ks], k_ref[pl.ds(r0, blk), ks]], axis=0)
            vcat = jnp.concatenate([v_ref[pl.ds(rp, blk), vs], v_ref[pl.ds(r0, blk), vs]], axis=0)
            for pr in range(SWA_GROUP // 2):
                c0 = (g * SWA_GROUP + 2 * pr) * SWA_HEAD_DIM
                q2 = q_ref[pl.ds(r0, blk), c0:c0 + LANES]
                res = []
                for half in range(2):
                    r = 2 * pr + half
                    qm = jnp.where(low_half, q2, zero) if half == 0 else jnp.where(low_half, zero, q2)
                    s = lax.dot_general(qm, kcat, NT_DIMS, preferred_element_type=f32)
                    s = s + bias_scr[first, g * SWA_GROUP + r]
                    sink = sink_ref[g * SWA_GROUP + r]
                    mx = jnp.maximum(jnp.max(s, axis=-1, keepdims=True), sink)
                    p = jnp.exp2(s - mx)
                    o = jnp.dot(p.astype(bf16), vcat, preferred_element_type=f32)
                    es = jnp.exp2(sink - mx)
                    lo, hi = o[:, :LANES], o[:, LANES:]
                    res.append(lo / (hi + es) if half == 0 else hi / (lo + es))
                o_ref[pl.ds(r0, blk), c0:c0 + LANES] = jnp.where(low_half, res[0], res[1]).astype(o_ref.dtype)
        return carry

    lax.fori_loop(0, nblk, block, 0)


def _swa_prompt(q, kd, v4, sinks_log2, nbatch, t):
    nq, nk, nv = q.shape[1], kd.shape[1], v4.shape[1]
    bias_shape = (2, SWA_HEADS, SWA_BLOCK, 2 * SWA_BLOCK)
    est = (2 * (2 * _nbytes((t, nq), bf16) + _nbytes((t, nk), bf16) + _nbytes((t, nv), bf16)) + _nbytes(bias_shape, f32)
           + 32 * _nbytes((SWA_BLOCK, 2 * SWA_BLOCK), f32))
    body = functools.partial(_swa_prompt_body, nblk=t // SWA_BLOCK)
    rows = lambda width: pl.BlockSpec((t, width), lambda b: (b, 0))
    return pl.pallas_call(
        body, out_shape=SDS(q.shape, bf16), grid=(nbatch,),
        in_specs=[pl.BlockSpec(memory_space=pltpu.SMEM), rows(nq), rows(nk), rows(nv)],
        out_specs=rows(nq), scratch_shapes=[pltpu.VMEM(bias_shape, f32)],
        compiler_params=_params(("arbitrary",), est), name="swa_prompt")(sinks_log2, q, kd, v4)


def _swa_decode_body(q_ref, kn_ref, vn_ref, ck_ref, cv_ref, slope_ref, sink_ref, o_ref, ok_ref, ov_ref, *, bb, w_buf):
    flat = w_buf * SWA_KV_HEADS
    col = lax.broadcasted_iota(jnp.int32, (SWA_HEADS, flat), 1)
    row = lax.broadcasted_iota(jnp.int32, (SWA_HEADS, flat), 0)
    own = (col % SWA_KV_HEADS) == (row // SWA_GROUP)
    dist = (w_buf - col // SWA_KV_HEADS).astype(f32)
    bias = jnp.where(own, -slope_ref[:, 0:1] * dist, MASKED)
    sink = sink_ref[:, 0:1]
    row_kv_d = lax.broadcasted_iota(jnp.int32, (SWA_HEADS, SWA_HEAD_DIM), 0) // SWA_GROUP

    def own_rows(per_kv):
        out = jnp.zeros((SWA_HEADS, SWA_HEAD_DIM), f32)
        for g in range(SWA_KV_HEADS):
            out = jnp.where(row_kv_d == g, per_kv[g:g + 1, :], out)
        return out

    for b in range(bb):
        q = q_ref[b].astype(bf16)
        kn, vn = kn_ref[b], vn_ref[b]
        k = ck_ref[b].reshape(flat, SWA_HEAD_DIM).astype(bf16)
        v = cv_ref[b].reshape(flat, SWA_HEAD_DIM).astype(bf16)
        s = lax.dot_general(q, k, NT_DIMS, preferred_element_type=f32) + bias
        sn = jnp.sum(q.astype(f32) * own_rows(kn).astype(bf16).astype(f32), axis=-1, keepdims=True)
        mx = jnp.maximum(jnp.maximum(jnp.max(s, axis=-1, keepdims=True), sn), sink)
        p = jnp.exp(s - mx)
        pn = jnp.exp(sn - mx)
        den = jnp.sum(p, axis=-1, keepdims=True) + pn + jnp.exp(sink - mx)
        o = jnp.dot(p.astype(bf16), v, preferred_element_type=f32)
        o = o + pn.astype(bf16).astype(f32) * own_rows(vn).astype(bf16).astype(f32)
        o_ref[b] = (o / den).astype(o_ref.dtype)
        ok_ref[b, 0:w_buf - 1] = ck_ref[b, 1:w_buf]
        ok_ref[b, w_buf - 1] = kn
        ov_ref[b, 0:w_buf - 1] = cv_ref[b, 1:w_buf]
        ov_ref[b, w_buf - 1] = vn


def _swa_decode(q, k_new, v_new, cache_k, cache_v, slopes, sinks, bb):
    nb, w_buf, kvh, hd = cache_k.shape
    padded = _nbytes((bb, w_buf, SUBLANES, LANES), f32)
    est = 2 * 4 * padded + (2 << 20)
    body = functools.partial(_swa_decode_body, bb=bb, w_buf=w_buf)
    cache_spec = pl.BlockSpec((bb, w_buf, kvh, hd), lambda i: (i, 0, 0, 0))
    new_spec = pl.BlockSpec((bb, kvh, hd), lambda i: (i, 0, 0))
    q_spec = pl.BlockSpec((bb, SWA_HEADS, hd), lambda i: (i, 0, 0))
    return pl.pallas_call(
        body, out_shape=[SDS((nb, SWA_HEADS, hd), bf16), SDS(cache_k.shape, cache_k.dtype), SDS(cache_v.shape, cache_v.dtype)],
        grid=(nb // bb,),
        in_specs=[q_spec, new_spec, new_spec, cache_spec, cache_spec,
                  _resident((SWA_HEADS, LANES)), _resident((SWA_HEADS, LANES))],
        out_specs=[q_spec, cache_spec, cache_spec],
        compiler_params=_params(("parallel",), est), name="swa_decode")(q, k_new, v_new, cache_k, cache_v, slopes, sinks)


def _mem_decode_body(q_ref, k_ref, v_ref, o_ref, *, bb):
    rows = q_ref.shape[1]
    flat = k_ref.shape[1] * MEM_HEADS
    col = lax.broadcasted_iota(jnp.int32, (rows, flat), 1)
    row = lax.broadcasted_iota(jnp.int32, (rows, flat), 0)
    own = (col % MEM_HEADS) == (row % MEM_HEADS)
    for b in range(bb):
        k = k_ref[b].reshape(flat, MEM_HEAD_DIM).astype(bf16)
        v = v_ref[b].reshape(flat, MEM_HEAD_DIM).astype(bf16)
        s = lax.dot_general(q_ref[b], k, NT_DIMS, preferred_element_type=f32)
        s = jnp.where(own, s, MASKED)
        p = jnp.exp(s - jnp.max(s, axis=-1, keepdims=True))
        den = jnp.sum(p, axis=-1, keepdims=True)
        o = jnp.dot(p.astype(bf16), v, preferred_element_type=f32)
        o_ref[b] = (o / den).astype(o_ref.dtype)


def _mem_decode(q, k, v, bb):
    nb, rows, hd = q.shape
    kv_block = (bb,) + k.shape[1:]
    est = 2 * (2 * _nbytes(kv_block, f32) + 2 * _nbytes((bb, rows, hd), bf16)) + 3 * _nbytes(kv_block, bf16) // bb + (4 << 20)
    body = functools.partial(_mem_decode_body, bb=bb)
    q_spec = pl.BlockSpec((bb, rows, hd), lambda i: (i, 0, 0))
    kv_spec = pl.BlockSpec(kv_block, lambda i: (i, 0, 0, 0))
    return pl.pallas_call(
        body, out_shape=SDS(q.shape, bf16), grid=(nb // bb,), in_specs=[q_spec, kv_spec, kv_spec], out_specs=q_spec,
        compiler_params=_params(("parallel",), est), name="mem_decode")(q, k, v)


def _mem_attn_body(q_ref, k_ref, v_ref, o_ref, *, bb):
    for b in range(bb):
        for h in range(MEM_HEADS):
            hs = slice(h * MEM_HEAD_DIM, (h + 1) * MEM_HEAD_DIM)
            q = q_ref[b, :, hs]
            k = k_ref[b, :, hs].astype(bf16)
            v = v_ref[b, :, hs].astype(bf16)
            s = lax.dot_general(q, k, NT_DIMS, preferred_element_type=f32)
            p = jnp.exp(s - jnp.max(s, axis=-1, keepdims=True))
            den = jnp.sum(p, axis=-1, keepdims=True)
            o = jnp.dot(p.astype(bf16), v, preferred_element_type=f32)
            o_ref[b, :, hs] = (o / den).astype(o_ref.dtype)


def _mem_attn(q, k, v, bb, tq):
    nb, t, md = q.shape
    mt = k.shape[1]
    kv_spec = pl.BlockSpec((bb, mt, md), lambda i, j: (i, 0, 0))
    est = 2 * (2 * _nbytes((bb, tq, md), bf16) + 2 * _nbytes((bb, mt, md), f32)) + 4 * _nbytes((tq, mt), f32) + (2 << 20)
    body = functools.partial(_mem_attn_body, bb=bb)
    return pl.pallas_call(
        body, out_shape=SDS((nb, t, md), bf16), grid=(nb // bb, t // tq),
        in_specs=[pl.BlockSpec((bb, tq, md), lambda i, j: (i, j, 0)), kv_spec, kv_spec],
        out_specs=pl.BlockSpec((bb, tq, md), lambda i, j: (i, j, 0)),
        compiler_params=_params(("parallel", "arbitrary"), est), name="mem_attn")(q, k, v)


def _ssd_gate_norm(y, z, gain):
    y = y * z
    gsz = y.shape[1] // SSD_GROUPS
    outs = []
    for g in range(SSD_GROUPS):
        yg = y[:, g * gsz:(g + 1) * gsz]
        ms = jnp.mean(yg * yg, axis=-1, keepdims=True)
        outs.append(yg * lax.rsqrt(ms + EPS) * gain[:, g * gsz:(g + 1) * gsz])
    return jnp.concatenate(outs, axis=-1)


def _ssd_prompt_body(xc_ref, z_ref, dt_ref, dtb_ref, alog_ref, dsk_ref, gn_ref, rexp_ref, ltri_ref,
                     y_ref, hout_ref, h_scr):
    c = pl.program_id(1)

    @pl.when(c == 0)
    def _():
        h_scr[...] = jnp.zeros_like(h_scr)

    for r0 in range(0, z_ref.shape[0], SSD_CHUNK):
        _ssd_chunk(xc_ref, z_ref, dt_ref, dtb_ref, alog_ref, dsk_ref, gn_ref, rexp_ref, ltri_ref, y_ref, h_scr, r0)

    @pl.when(c == pl.num_programs(1) - 1)
    def _():
        hout_ref[0] = h_scr[...].T


def _ssd_chunk(xc_ref, z_ref, dt_ref, dtb_ref, alog_ref, dsk_ref, gn_ref, rexp_ref, ltri_ref, y_ref, h_scr, r0):
    chunk = SSD_CHUNK
    di = z_ref.shape[1]
    gn_w = SSD_GROUPS * SSD_D_STATE
    gw = di // SSD_GROUPS
    rs = slice(r0, r0 + chunk)

    xs = xc_ref[rs, :di].astype(f32)
    bm = xc_ref[rs, di:di + gn_w]
    cm = xc_ref[rs, di + gn_w:]

    dt = _softplus(dt_ref[rs, :] + dtb_ref[...])
    da = dt * (-jnp.exp(alog_ref[...]))
    acs = jnp.dot(ltri_ref[...], da, precision=lax.Precision.HIGHEST, preferred_element_type=f32)
    acs_t = acs.T
    dt_t = dt.T
    rexp = rexp_ref[...]
    eacs_e = _split_dot(jnp.exp(acs), rexp)
    xw = (xs * _split_dot(jnp.exp(acs[chunk - 1:chunk, :] - acs) * dt, rexp)).astype(bf16)

    row = lax.broadcasted_iota(jnp.int32, (chunk, chunk), 0)
    col = lax.broadcasted_iota(jnp.int32, (chunk, chunk), 1)
    causal = row >= col
    low_half = col < SSD_HEAD_DIM
    zero = jnp.zeros((chunk, LANES), bf16)
    heads_per_group = SSD_HEADS // SSD_GROUPS
    ys = []
    for g in range(SSD_GROUPS):
        ns = slice(g * SSD_D_STATE, (g + 1) * SSD_D_STATE)
        ls = slice(g * gw, (g + 1) * gw)
        cb = lax.dot_general(cm[:, ns], bm[:, ns], NT_DIMS, preferred_element_type=f32)
        hg = h_scr[:, ls]
        y_off = jnp.dot(cm[:, ns], hg.astype(bf16), preferred_element_type=f32) * eacs_e[:, ls]
        y_diag = []
        for pr in range(heads_per_group // 2):
            h0 = g * heads_per_group + 2 * pr
            xp = xc_ref[rs, h0 * SSD_HEAD_DIM:h0 * SSD_HEAD_DIM + LANES]
            ws = []
            for hh in (h0, h0 + 1):
                diff = acs[:, hh:hh + 1] - acs_t[hh:hh + 1, :]
                ws.append((cb * jnp.exp(jnp.where(causal, diff, MASKED)) * dt_t[hh:hh + 1, :]).astype(bf16))
            x2 = jnp.concatenate([jnp.where(low_half, xp, zero), jnp.where(low_half, zero, xp)], axis=0)
            y_diag.append(jnp.dot(jnp.concatenate(ws, axis=1), x2, preferred_element_type=f32))
        st = lax.dot_general(bm[:, ns], xw[:, ls], TN_DIMS, preferred_element_type=f32)
        h_scr[:, ls] = eacs_e[chunk - 1:chunk, ls] * hg + st
        ys.append(jnp.concatenate(y_diag, axis=-1) + y_off)
    y = jnp.concatenate(ys, axis=-1) + dsk_ref[...] * xs
    y_ref[rs, :] = _ssd_gate_norm(y, z_ref[rs, :].astype(f32), gn_ref[...]).astype(y_ref.dtype)


def _ssd_prompt(xc, z, dt, ssd_w, nbatch, t, chunks_per_step):
    _, _, dtb, alog, dsk, gn, rexp, ltri = ssd_w
    ch, di = xc.shape[1], z.shape[1]
    tr = chunks_per_step * SSD_CHUNK
    nstep = t // tr
    rows = lambda b, c: (b * nstep + c, 0)
    est = (2 * (_nbytes((tr, ch), bf16) + 2 * _nbytes((tr, di), bf16) + _nbytes((di, SSD_D_STATE), f32))
           + _nbytes((SSD_D_STATE, di), f32) + 24 * _nbytes((SSD_CHUNK, ch), f32))
    return pl.pallas_call(
        _ssd_prompt_body,
        out_shape=[SDS((nbatch * t, di), bf16), SDS((nbatch, di, SSD_D_STATE), f32)],
        grid=(nbatch, nstep),
        in_specs=[pl.BlockSpec((tr, ch), rows), pl.BlockSpec((tr, di), rows),
                  pl.BlockSpec((tr, LANES), rows),
                  _resident(dtb.shape), _resident(alog.shape),
                  _resident(dsk.shape), _resident(gn.shape), _resident(rexp.shape), _resident(ltri.shape)],
        out_specs=[pl.BlockSpec((tr, di), rows), pl.BlockSpec((1, di, SSD_D_STATE), lambda b, c: (b, 0, 0))],
        scratch_shapes=[pltpu.VMEM((SSD_D_STATE, di), f32)],
        compiler_params=_params(("parallel", "arbitrary"), est), name="ssd_prompt")(xc, z, dt, dtb, alog, dsk, gn, rexp, ltri)


def _ssd_step_body(xbc_ref, z_ref, dt_ref, cs_ref, h_ref, cw_ref, cb_ref, dtb_ref, alog_ref, dsk_ref, gn_ref, rexp_ref,
                   y_ref, cso_ref, ho_ref, xdt_scr, da_scr, bm_scr, cm_scr, y_scr, *, bb):
    ch = xbc_ref.shape[1]
    di = z_ref.shape[1]
    gn_w = SSD_GROUPS * SSD_D_STATE
    gw = di // SSD_GROUPS
    halo = SSD_CONV - 1
    xr = xbc_ref[...]
    conv = cb_ref[...] + cw_ref[halo:halo + 1, :] * xr
    for k in range(halo):
        conv = conv + cw_ref[k:k + 1, :] * cs_ref[:, k * ch:(k + 1) * ch]
    for k in range(1, halo):
        cso_ref[:, (k - 1) * ch:k * ch] = cs_ref[:, k * ch:(k + 1) * ch]
    cso_ref[:, (halo - 1) * ch:] = xr
    xc = _silu(conv)
    xs = xc[:, :di]
    dt = _softplus(dt_ref[...] + dtb_ref[...])
    rexp = rexp_ref[...]
    xdt_scr[...] = xs * _split_dot(dt, rexp)
    da_scr[...] = _split_dot(jnp.exp(dt * (-jnp.exp(alog_ref[...]))), rexp)
    bm_scr[...] = xc[:, di:di + gn_w]
    cm_scr[...] = xc[:, di + gn_w:]

    rows = 2 * SUBLANES
    rowi = lax.broadcasted_iota(jnp.int32, (rows, di), 0)
    grp = lax.broadcasted_iota(jnp.int32, (rows, di), 1) // gw
    rown = lax.broadcasted_iota(jnp.int32, (rows, SSD_D_STATE), 0)
    ones_rows = jnp.where((rown == SSD_GROUPS) | (rown == SSD_GROUPS + 1), 1.0, 0.0).astype(f32)

    def per_row(b, carry):
        xrow = xdt_scr[pl.ds(b, 1), :]
        drow = da_scr[pl.ds(b, 1), :]
        d_hi = drow.astype(bf16).astype(f32)
        lhs = jnp.where(rowi == grp, xrow, jnp.where(rowi == SSD_GROUPS, d_hi, jnp.where(rowi == SSD_GROUPS + 1, drow - d_hi, 0.0)))
        brow = bm_scr[pl.ds(b, 1), :]
        crow = cm_scr[pl.ds(b, 1), :]
        rhs_b = jnp.zeros((rows, SSD_D_STATE), f32)
        c_rows = jnp.zeros((rows, SSD_D_STATE), f32)
        for g in range(SSD_GROUPS):
            ns = slice(g * SSD_D_STATE, (g + 1) * SSD_D_STATE)
            rhs_b = jnp.where(rown == g, brow[:, ns], rhs_b)
            c_rows = jnp.where(rown == g, crow[:, ns], c_rows)
        rhs = jnp.concatenate([rhs_b, ones_rows], axis=1).astype(bf16)
        sd = lax.dot_general(lhs.astype(bf16), rhs, TN_DIMS, preferred_element_type=f32)
        hn = sd[:, SSD_D_STATE:] * h_ref[b] + sd[:, :SSD_D_STATE]
        ho_ref[b] = hn
        y8 = lax.dot_general(c_rows.astype(bf16), hn.astype(bf16), NT_DIMS, preferred_element_type=f32)
        y_scr[pl.ds(b, 1), :] = jnp.sum(jnp.where(rowi == grp, y8, 0.0), axis=0, keepdims=True)
        return carry

    lax.fori_loop(0, bb, per_row, 0)
    y = y_scr[...] + dsk_ref[...] * xs
    y_ref[...] = _ssd_gate_norm(y, _silu(z_ref[...]), gn_ref[...]).astype(y_ref.dtype)


def _ssd_step(xbc, z, dt, conv_state, h0, ssd_w, bb):
    cw, cb, dtb, alog, dsk, gn, rexp, _ = ssd_w
    nb, ch = xbc.shape
    di = z.shape[1]
    gn_w = SSD_GROUPS * SSD_D_STATE
    halo = SSD_CONV - 1
    est = (2 * (2 * _nbytes((bb, di, SSD_D_STATE), f32) + 2 * _nbytes((bb, halo * ch), f32) + 4 * _nbytes((bb, ch), f32))
           + 8 * _nbytes((di, 2 * SSD_D_STATE), f32))
    body = functools.partial(_ssd_step_body, bb=bb)
    r2 = lambda i: (i, 0)
    return pl.pallas_call(
        body,
        out_shape=[SDS((nb, di), bf16), SDS((nb, halo * ch), f32), SDS((nb, di, SSD_D_STATE), f32)],
        grid=(nb // bb,),
        in_specs=[pl.BlockSpec((bb, ch), r2), pl.BlockSpec((bb, di), r2), pl.BlockSpec((bb, LANES), r2),
                  pl.BlockSpec((bb, halo * ch), r2), pl.BlockSpec((bb, di, SSD_D_STATE), lambda i: (i, 0, 0)),
                  _resident(cw.shape), _resident(cb.shape), _resident(dtb.shape), _resident(alog.shape),
                  _resident(dsk.shape), _resident(gn.shape), _resident(rexp.shape)],
        out_specs=[pl.BlockSpec((bb, di), r2), pl.BlockSpec((bb, halo * ch), r2),
                   pl.BlockSpec((bb, di, SSD_D_STATE), lambda i: (i, 0, 0))],
        scratch_shapes=[pltpu.VMEM((bb, di), f32), pltpu.VMEM((bb, di), f32), pltpu.VMEM((bb, gn_w), f32),
                        pltpu.VMEM((bb, gn_w), f32), pltpu.VMEM((bb, di), f32)],
        compiler_params=_params(("parallel",), est), name="ssd_step")(xbc, z, dt, conv_state, h0, cw, cb, dtb, alog, dsk, gn, rexp)


def _merge_body(a_ref, s_ref, m_ref, g0_ref, g1_ref, g2_ref, wa_ref, ws_ref, wm_ref, o_ref, *, chunk):
    a, s, m = a_ref[...], s_ref[...], m_ref[...]
    for c0 in range(0, o_ref.shape[1], chunk):
        cs = slice(c0, c0 + chunk)
        acc = g0_ref[:, cs].astype(f32) * jnp.dot(a, wa_ref[:, cs], preferred_element_type=f32)
        acc = acc + g1_ref[:, cs].astype(f32) * jnp.dot(s, ws_ref[:, cs], preferred_element_type=f32)
        acc = acc + g2_ref[:, cs].astype(f32) * jnp.dot(m, wm_ref[:, cs], preferred_element_type=f32)
        o_ref[:, cs] = acc.astype(o_ref.dtype)


def _merge(a, s, mo, gates, wa, ws, wm, tm):
    m, d = s.shape[0], wa.shape[1]
    chunk = 512
    est = (2 * (_nbytes((tm, a.shape[1]), bf16) + _nbytes((tm, s.shape[1]), bf16) + _nbytes((tm, mo.shape[1]), bf16)
                + 4 * _nbytes((tm, d), bf16))
           + _nbytes(wa.shape, bf16) + _nbytes(ws.shape, bf16) + _nbytes(wm.shape, bf16) + 6 * _nbytes((tm, chunk), f32))
    body = functools.partial(_merge_body, chunk=chunk)
    return pl.pallas_call(
        body, out_shape=SDS((m, d), bf16), grid=(m // tm,),
        in_specs=[pl.BlockSpec((tm, a.shape[1]), lambda i: (i, 0)), pl.BlockSpec((tm, s.shape[1]), lambda i: (i, 0)),
                  pl.BlockSpec((tm, mo.shape[1]), lambda i: (i, 0)),
                  pl.BlockSpec((tm, d), lambda i: (i, 0)), pl.BlockSpec((tm, d), lambda i: (i, 1)),
                  pl.BlockSpec((tm, d), lambda i: (i, 2)),
                  _resident(wa.shape), _resident(ws.shape), _resident(wm.shape)],
        out_specs=pl.BlockSpec((tm, d), lambda i: (i, 0)),
        compiler_params=_params(("parallel",), est), name="merge")(a, s, mo, gates, gates, gates, wa, ws, wm)


def _ffn_body(x_ref, g_ref, wg_ref, wu_ref, wd_ref, o_ref, h_scr):
    @pl.when(pl.program_id(1) == 0)
    def _():
        x = x_ref[...]
        ms = jnp.mean(x * x, axis=-1, keepdims=True)
        h_scr[...] = (x * lax.rsqrt(ms + EPS) * g_ref[...]).astype(bf16)
        o_ref[...] = x

    h = h_scr[...]
    gate = jnp.dot(h, wg_ref[...], preferred_element_type=f32)
    up = jnp.dot(h, wu_ref[...], preferred_element_type=f32)
    act = (_silu(gate) * up).astype(bf16)
    chunk = 512
    for n0 in range(0, o_ref.shape[1], chunk):
        o_ref[:, n0:n0 + chunk] += jnp.dot(act, wd_ref[:, n0:n0 + chunk], preferred_element_type=f32)


def _ffn(x, gain, wg, wu, wd, tm, tf):
    m, d = x.shape
    dff = wg.shape[1]
    est = (4 * _nbytes((tm, d), f32) + _nbytes((tm, d), bf16) + 2 * 3 * _nbytes((d, tf), bf16)
           + 4 * _nbytes((tm, tf), f32))
    return pl.pallas_call(
        _ffn_body, out_shape=SDS((m, d), f32), grid=(m // tm, dff // tf),
        in_specs=[pl.BlockSpec((tm, d), lambda i, c: (i, 0)), _resident((1, d)),
                  pl.BlockSpec((d, tf), lambda i, c: (0, c)), pl.BlockSpec((d, tf), lambda i, c: (0, c)),
                  pl.BlockSpec((tf, d), lambda i, c: (c, 0))],
        out_specs=pl.BlockSpec((tm, d), lambda i, c: (i, 0)),
        scratch_shapes=[pltpu.VMEM((tm, d), bf16)],
        compiler_params=_params(("parallel", "arbitrary"), est), name="ffn")(x, gain.reshape(1, d), wg, wu, wd)


def _dup_heads(w):
    lead = w.shape[:-1]
    w = w.reshape(*lead, SWA_KV_HEADS, 1, SWA_HEAD_DIM)
    return jnp.broadcast_to(w, (*lead, SWA_KV_HEADS, 2, SWA_HEAD_DIM)).reshape(*lead, SWA_KV_HEADS * LANES)


def _pad_lanes(v, n=LANES):
    return jnp.pad(v, [(0, 0)] * (v.ndim - 1) + [(0, n - v.shape[-1])])


def _layer(xp, xs, cache_k, cache_v, cmem_k, cmem_v, state_ssm, state_conv, mem_prompt, w):
    bp, tp, d = xp.shape
    bs = xs.shape[0]
    w_buf = cache_k.shape[1]
    assert xs.shape[1] == 1 and w_buf == WINDOW and tp % SSD_CHUNK == 0 and bs % SUBLANES == 0
    q_dim = SWA_HEADS * SWA_HEAD_DIM
    kv_dim = SWA_KV_HEADS * SWA_HEAD_DIM
    di = SSD_HEADS * SSD_HEAD_DIM
    ch = di + 2 * SSD_GROUPS * SSD_D_STATE
    md = MEM_HEADS * MEM_HEAD_DIM
    o = 0
    w_in = w["w_in"].astype(bf16)
    wq, o = w_in[:, o:o + q_dim], o + q_dim
    wk, o = w_in[:, o:o + kv_dim], o + kv_dim
    wv, o = w_in[:, o:o + kv_dim], o + kv_dim
    wz, o = w_in[:, o:o + di], o + di
    wxbc, o = w_in[:, o:o + ch], o + ch
    wdt, o = w_in[:, o:o + SSD_HEADS], o + SSD_HEADS
    wqm, o = w_in[:, o:o + md], o + md
    wg = w_in[:, o:]
    wdt = _pad_lanes(wdt)

    q_scale = SWA_HEAD_DIM ** -0.5
    gq = jnp.tile(w["q_norm_swa"], SWA_HEADS) * q_scale
    gk = jnp.tile(w["k_norm_swa"], SWA_KV_HEADS)
    head_of = jnp.arange(2 * LANES) // SWA_HEAD_DIM
    blockdiag = (head_of[:, None] == head_of[None, :]).astype(bf16)
    w_qkv_p = jnp.concatenate([wq, _dup_heads(wk), _dup_heads(wv), wdt], axis=1)
    w_qkv_s = jnp.concatenate([wq, wk, wv, wdt], axis=1)
    gqm = (jnp.tile(w["q_norm_mem"], MEM_HEADS) * MEM_HEAD_DIM ** -0.5).reshape(1, md)
    gkm = jnp.tile(w["k_norm_mem"], MEM_HEADS).reshape(1, md)
    wz_b, wxbc_b, wqm_b, wg_b = wz, wxbc, wqm, wg
    w_mem_k, w_mem_v = w["w_mem_kv"][:, :md].astype(bf16), w["w_mem_kv"][:, md:].astype(bf16)
    wa, ws, wm = w["w_up_swa"].astype(bf16), w["w_up_ssd"].astype(bf16), w["w_up_mem"].astype(bf16)
    w_out = w["w_out"].astype(bf16)
    w_gate, w_up, w_down = w["w_gate"].astype(bf16), w["w_up"].astype(bf16), w["w_down"].astype(bf16)

    head_of_lane = jnp.arange(di) // SSD_HEAD_DIM
    rexp = (jnp.arange(LANES)[:, None] == head_of_lane[None, :]).astype(bf16)
    ltri = (jnp.arange(SSD_CHUNK)[:, None] >= jnp.arange(SSD_CHUNK)[None, :]).astype(f32)
    ssd_w = (w["conv_w"], w["conv_b"].reshape(1, ch), _pad_lanes(w["dt_bias"].reshape(1, -1)),
             _pad_lanes(w["a_log"].reshape(1, -1)), jnp.repeat(w["d_skip"], SSD_HEAD_DIM).reshape(1, di),
             w["ssd_norm"].reshape(1, di), rexp, ltri)
    slopes = jnp.broadcast_to(jnp.exp2(-8.0 * jnp.arange(1, SWA_HEADS + 1, dtype=f32) / SWA_HEADS)[:, None], (SWA_HEADS, LANES))
    sinks_b = jnp.broadcast_to(w["swa_sinks"].astype(f32)[:, None], (SWA_HEADS, LANES))

    def finish(x2, a_out, s_out, m_out, gates, tm, tm_ffn):
        merged = _merge(a_out, s_out, m_out, gates, wa, ws, wm, tm)
        x1 = _proj(merged, w_out, _epi_residual, f32, tm, d, aux=(x2,), aux_specs=(pl.BlockSpec((tm, d), lambda i, j: (i, 0)),),
                   name="out_proj")
        return _ffn(x1, w["norm_ffn"], w_gate, w_up, w_down, tm_ffn, 512)

    mp = bp * tp
    x2 = xp.reshape(mp, d)
    tm = min(1024, tp)
    tpb = tp // tm
    tm_qkv = min(512, tp)
    g_mix = w["norm_mix"].reshape(1, d)
    h, q, kd, v4, dt, klast, vlast = _qkv_proj(
        x2, g_mix, w_qkv_p, (gq * LOG2_E).reshape(1, -1), _dup_heads(gk).reshape(1, -1), blockdiag,
        q_dim, 2 * kv_dim, 2 * kv_dim, tm_qkv, tp // tm_qkv, bf16, bf16, True)
    z = _proj(h, wz_b, _epi_silu, bf16, tm, di, name="z_proj")
    xc, xtail = _xbc_proj(h, wxbc_b, ssd_w[0], ssd_w[1], tm, tpb)
    qm = _proj(h, wqm_b, _epi_headnorm256, bf16, tm, md, aux=(gqm,), aux_specs=(_resident((1, md)),), name="qm_proj")
    gates = _proj(h, wg_b, _epi_sigmoid, bf16, tm, d, name="gate_proj")
    a_out = _swa_prompt(q, kd, v4, w["swa_sinks"].astype(f32) * LOG2_E, bp, tp)
    s_out, p_h = _ssd_prompt(xc, z, dt, ssd_w, bp, tp, 4 if tp % (4 * SSD_CHUNK) == 0 else 1)
    mt = mem_prompt.shape[1]
    hm = _rmsnorm(mem_prompt.reshape(bp * mt, d), w["norm_mem"], mt)
    mk = _proj(hm, w_mem_k, _epi_headnorm256, f32, mt, md, aux=(gkm,), aux_specs=(_resident((1, md)),), name="mem_k_proj")
    mv = _proj(hm, w_mem_v, _epi_plain, f32, mt, md, name="mem_v_proj")
    m_out = _mem_attn(qm.reshape(bp, tp, md), mk.reshape(bp, mt, md), mv.reshape(bp, mt, md), 1, tm)
    yp = finish(x2, a_out, s_out, m_out.reshape(mp, md), gates, min(512, tp), min(512, tp)).reshape(bp, tp, d)

    undup = lambda t: t.reshape(bp, WINDOW, SWA_KV_HEADS, 2, SWA_HEAD_DIM)[:, :, :, 0, :]
    p_k, p_v = undup(klast), undup(vlast)
    p_mk = mk.reshape(bp, mt, MEM_HEADS, MEM_HEAD_DIM)
    p_mv = mv.reshape(bp, mt, MEM_HEADS, MEM_HEAD_DIM)
    p_h = p_h.reshape(bp, SSD_HEADS, SSD_HEAD_DIM, SSD_D_STATE)
    p_c = xtail.reshape(bp, SUBLANES, ch)[:, SUBLANES - (SSD_CONV - 1):, :]

    xs2 = xs.reshape(bs, d)
    hs, qs, ks, vs, dts = _qkv_proj(xs2, g_mix, w_qkv_s, gq.reshape(1, -1), gk.reshape(1, -1), blockdiag,
                                    q_dim, kv_dim, kv_dim, bs, 1, f32, f32, False)
    zs = _proj(hs, wz_b, _epi_plain, f32, bs, di, name="z_proj_s")
    xbcs = _proj(hs, wxbc_b, _epi_plain, f32, bs, ch, name="xbc_proj_s")
    qms = _proj(hs, wqm_b, _epi_headnorm256, bf16, bs, md, aux=(gqm,), aux_specs=(_resident((1, md)),), name="qm_proj_s")
    gates_s = _proj(hs, wg_b, _epi_sigmoid, bf16, bs, d, name="gate_proj_s")
    a_s, s_k, s_v = _swa_decode(qs.reshape(bs, SWA_HEADS, SWA_HEAD_DIM), ks.reshape(bs, SWA_KV_HEADS, SWA_HEAD_DIM),
                                vs.reshape(bs, SWA_KV_HEADS, SWA_HEAD_DIM), cache_k, cache_v, slopes, sinks_b, SUBLANES)
    halo = SSD_CONV - 1
    s_s, s_c, s_h = _ssd_step(xbcs, zs, dts, state_conv.reshape(bs, halo * ch), state_ssm.reshape(bs, di, SSD_D_STATE),
                              ssd_w, SUBLANES)
    qm_rows = jnp.tile(qms.reshape(bs, MEM_HEADS, MEM_HEAD_DIM), (1, 2 * SUBLANES // MEM_HEADS, 1))
    m_s = _mem_decode(qm_rows, cmem_k, cmem_v, 2)[:, :MEM_HEADS, :].reshape(bs, md)
    ys = finish(xs2, a_s.reshape(bs, q_dim), s_s, m_s, gates_s, bs, bs).reshape(bs, 1, d)

    s_h =s_h.reshape(bs, SSD_HEADS, SSD_HEAD_DIM, SSD_D_STATE)
    s_c = s_c.reshape(bs, halo, ch)
    return yp, ys, (p_k, p_v, p_mk, p_mv, p_h, p_c), (s_k, s_v, s_h, s_c)


_WEIGHT_NAMES = ("norm_mix", "w_in", "q_norm_swa", "k_norm_swa", "swa_sinks", "conv_w", "conv_b", "dt_bias", "a_log",
                 "d_skip", "ssd_norm", "norm_mem", "w_mem_kv", "q_norm_mem", "k_norm_mem", "w_up_swa", "w_up_ssd",
                 "w_up_mem", "w_out", "norm_ffn", "w_gate", "w_up", "w_down")


def kernel(x_prompt, x_sample, cache_swa_k, cache_swa_v, cache_mem_k, cache_mem_v, state_ssm, state_conv, mem_prompt, norm_mix, w_in, q_norm_swa, k_norm_swa, swa_sinks, conv_w, conv_b, dt_bias, a_log, d_skip, ssd_norm, norm_mem, w_mem_kv, q_norm_mem, k_norm_mem, w_up_swa, w_up_ssd, w_up_mem, w_out, norm_ffn, w_gate, w_up, w_down):
    weights = (norm_mix, w_in, q_norm_swa, k_norm_swa, swa_sinks, conv_w, conv_b, dt_bias, a_log, d_skip, ssd_norm,
               norm_mem, w_mem_kv, q_norm_mem, k_norm_mem, w_up_swa, w_up_ssd, w_up_mem, w_out, norm_ffn, w_gate, w_up, w_down)
    depth = w_in.shape[0]
    layer = (lambda a, l: a.reshape(a.shape[1:])) if depth == 1 else (lambda a, l: a[l])
    yp, ys = x_prompt, x_sample
    p_outs, s_outs = [], []
    for l in range(depth):
        w = {n: layer(a, l) for n, a in zip(_WEIGHT_NAMES, weights)}
        yp, ys, po, so = _layer(yp, ys, layer(cache_swa_k, l), layer(cache_swa_v, l), layer(cache_mem_k, l),
                                layer(cache_mem_v, l), layer(state_ssm, l), layer(state_conv, l), mem_prompt, w)
        p_outs.append(po)
        s_outs.append(so)
    stack = lambda outs, i: jnp.stack([o[i] for o in outs])
    return (yp, ys, *(stack(p_outs, i) for i in range(6)), *(stack(s_outs, i) for i in range(4)))
```

```python
import functools

import jax
import jax.numpy as jnp
import numpy as np
from jax import lax
from jax.experimental import pallas as pl
from jax.experimental.pallas import tpu as pltpu

f32 = jnp.float32
bf16 = jnp.bfloat16
SDS = jax.ShapeDtypeStruct

LANES = 128
SUBLANES = 8
VMEM_BYTES_V7X = 64 * 1024 * 1024
VMEM_HEADROOM = 8 * 1024 * 1024

EPS = 1e-6
SWA_HEADS = 16
SWA_KV_HEADS = 4
SWA_GROUP = SWA_HEADS // SWA_KV_HEADS
SWA_HEAD_DIM = 64
WINDOW = 128
SWA_BLOCK = 128
SSD_HEAD_DIM = 64
SSD_HEADS = 32
SSD_GROUPS = 4
SSD_D_STATE = 128
SSD_CONV = 4
SSD_CHUNK = 128
MEM_HEADS = 4
MEM_HEAD_DIM = 256
N_BRANCH = 3
MASKED = -1e30
LOG2_E = 1.4426950408889634

NT_DIMS = (((1,), (1,)), ((), ()))
TN_DIMS = (((0,), (0,)), ((), ()))


def _params(semantics, block_bytes):
    limit = min(int(block_bytes) + VMEM_HEADROOM, VMEM_BYTES_V7X - VMEM_HEADROOM)
    return pltpu.CompilerParams(dimension_semantics=semantics, vmem_limit_bytes=limit)


def _nbytes(shape, dtype):
    n = 1
    for s in shape:
        n *= s
    return n * jnp.dtype(dtype).itemsize


def _resident(shape):
    nd = len(shape)
    return pl.BlockSpec(shape, lambda *_: (0,) * nd, pipeline_mode=pl.Buffered(1))


def _sigmoid(x):
    return 0.5 + 0.5 * jnp.tanh(0.5 * x)


def _silu(x):
    hx = 0.5 * x
    return hx + hx * jnp.tanh(hx)


def _softplus(x):
    return jnp.maximum(x, 0.0) + jnp.log1p(jnp.exp(-jnp.abs(x)))


def _split_dot(v, m):
    hi = v.astype(bf16)
    lo = (v - hi.astype(f32)).astype(bf16)
    return jnp.dot(hi, m, preferred_element_type=f32) + jnp.dot(lo, m, preferred_element_type=f32)


def _headnorm64(acc, gain, blockdiag):
    outs = []
    wide = blockdiag.shape[0]
    for c in range(acc.shape[1] // wide):
        a = acc[:, c * wide:(c + 1) * wide]
        ss = jnp.dot((a * a).astype(bf16), blockdiag, preferred_element_type=f32)
        outs.append(a * lax.rsqrt(ss * (1.0 / SWA_HEAD_DIM) + EPS) * gain[:, c * wide:(c + 1) * wide])
    return outs[0] if len(outs) == 1 else jnp.concatenate(outs, axis=-1)


def _norm_body(x_ref, g_ref, o_ref):
    x = x_ref[...]
    ms = jnp.mean(x * x, axis=-1, keepdims=True)
    o_ref[...] = (x * lax.rsqrt(ms + EPS) * g_ref[...]).astype(o_ref.dtype)


def _rmsnorm(x, gain, tm):
    m, d = x.shape
    est = 2 * (_nbytes((tm, d), f32) + _nbytes((tm, d), bf16)) + _nbytes((tm, d), f32)
    return pl.pallas_call(
        _norm_body, out_shape=SDS((m, d), bf16), grid=(m // tm,),
        in_specs=[pl.BlockSpec((tm, d), lambda i: (i, 0)), _resident((1, d))],
        out_specs=pl.BlockSpec((tm, d), lambda i: (i, 0)),
        compiler_params=_params(("parallel",), est), name="rmsnorm")(x, gain.reshape(1, d))


def _qkv_body(x_ref, gx_ref, w_ref, gq_ref, gk_ref, bd_ref, h_ref, q_ref, k_ref, *rest, nq, nk, nv, chunk, prompt):
    v_ref, dt_ref, *last_refs = rest
    x = x_ref[...]
    h = (x * lax.rsqrt(jnp.mean(x * x, axis=-1, keepdims=True) + EPS) * gx_ref[...]).astype(bf16)
    h_ref[...] = h
    bd = bd_ref[...]
    tm = h.shape[0]
    for c0 in range(0, nq, chunk):
        acc = jnp.dot(h, w_ref[:, c0:c0 + chunk], preferred_element_type=f32)
        q_ref[:, c0:c0 + chunk] = _headnorm64(acc, gq_ref[:, c0:c0 + chunk], bd).astype(q_ref.dtype)
    for c0 in range(0, nk, chunk):
        w = min(chunk, nk - c0)
        acc = jnp.dot(h, w_ref[:, nq + c0:nq + c0 + w], preferred_element_type=f32)
        kn = _headnorm64(acc, gk_ref[:, c0:c0 + w], bd)
        k_ref[:, c0:c0 + w] = kn.astype(k_ref.dtype)
        if last_refs:
            last_refs[0][:, c0:c0 + w] = kn[tm - WINDOW:, :]
    for c0 in range(0, nv, chunk):
        w = min(chunk, nv - c0)
        acc = jnp.dot(h, w_ref[:, nq + nk + c0:nq + nk + c0 + w], preferred_element_type=f32)
        if prompt:
            low_half = lax.broadcasted_iota(jnp.int32, (tm, LANES), 1) < SWA_HEAD_DIM
            for t0 in range(0, w, LANES):
                a = acc[:, t0:t0 + LANES]
                v_ref[:, 2 * (c0 + t0):2 * (c0 + t0) + LANES] = jnp.where(low_half, a, 1.0).astype(v_ref.dtype)
                v_ref[:, 2 * (c0 + t0) + LANES:2 * (c0 + t0 + LANES)] = jnp.where(low_half, 1.0, a).astype(v_ref.dtype)
            last_refs[1][:, c0:c0 + w] = acc[tm - WINDOW:, :]
        else:
            v_ref[:, c0:c0 + w] = acc.astype(v_ref.dtype)
    dt_ref[...] = jnp.dot(h, w_ref[:, nq + nk + nv:], preferred_element_type=f32)


def _qkv_proj(x, gx, w, gq, gk, blockdiag, nq, nk, nv, tm, tiles_per_batch, q_dtype, kv_dtype, prompt):
    m, d = x.shape
    n = w.shape[1]
    chunk = 512
    rows = lambda width: pl.BlockSpec((tm, width), lambda i: (i, 0))
    v_width = 2 * nv if prompt else nv
    out_shape = [SDS((m, d), bf16), SDS((m, nq), q_dtype), SDS((m, nk), kv_dtype), SDS((m, v_width), kv_dtype),
                 SDS((m, LANES), f32)]
    out_specs = [rows(d), rows(nq), rows(nk), rows(v_width), rows(LANES)]
    if prompt:
        nb = m // (tm * tiles_per_batch)
        out_shape += [SDS((nb * WINDOW, nk), f32), SDS((nb * WINDOW, nv), f32)]
        out_specs += [pl.BlockSpec((WINDOW, nk), lambda i: (i // tiles_per_batch, 0)),
                      pl.BlockSpec((WINDOW, nv), lambda i: (i // tiles_per_batch, 0))]
    est = (2 * _nbytes((tm, d), f32) + 3 * _nbytes((tm, d), bf16) + _nbytes((d, n), bf16)
           + 2 * _nbytes((tm, nq + nk + 2 * nv), f32) + 4 * _nbytes((tm, chunk), f32))
    body = functools.partial(_qkv_body, nq=nq, nk=nk, nv=nv, chunk=chunk, prompt=prompt)
    return pl.pallas_call(
        body, out_shape=out_shape, grid=(m // tm,),
        in_specs=[pl.BlockSpec((tm, d), lambda i: (i, 0)), _resident((1, d)), _resident((d, n)), _resident((1, nq)),
                  _resident((1, nk)), _resident(blockdiag.shape)],
        out_specs=out_specs, compiler_params=_params(("arbitrary",), est), name="qkv_proj")(x, gx, w, gq, gk, blockdiag)


def _proj_body(h_ref, w_ref, *refs, epilogue, chunk, n_aux):
    aux, outs = refs[:n_aux], refs[n_aux:]
    h = h_ref[...]
    tn = w_ref.shape[1]
    for c0 in range(0, tn, chunk):
        acc = jnp.dot(h, w_ref[:, c0:c0 + chunk], preferred_element_type=f32)
        epilogue(acc, c0, chunk, aux, outs)


def _epi_plain(acc, c0, w, aux, outs):
    outs[0][:, c0:c0 + w] = acc.astype(outs[0].dtype)


def _epi_silu(acc, c0, w, aux, outs):
    outs[0][:, c0:c0 + w] = _silu(acc).astype(outs[0].dtype)


def _epi_sigmoid(acc, c0, w, aux, outs):
    outs[0][:, c0:c0 + w] = _sigmoid(acc).astype(outs[0].dtype)


def _epi_headnorm256(acc, c0, w, aux, outs):
    gain = aux[0]
    for h0 in range(0, w, MEM_HEAD_DIM):
        a = acc[:, h0:h0 + MEM_HEAD_DIM]
        ms = jnp.mean(a * a, axis=-1, keepdims=True)
        y = a * lax.rsqrt(ms + EPS) * gain[:, c0 + h0:c0 + h0 + MEM_HEAD_DIM]
        outs[0][:, c0 + h0:c0 + h0 + MEM_HEAD_DIM] = y.astype(outs[0].dtype)


def _proj(h, w, epilogue, out_dtype, tm, tn, aux=(), aux_specs=(), chunk=512, name="proj"):
    m, d = h.shape
    n = w.shape[1]
    nj = n // tn
    chunk = min(chunk, tn)
    w_spec = _resident((d, n)) if nj == 1 else pl.BlockSpec((d, tn), lambda i, j: (0, j))
    est = (2 * _nbytes((tm, d), bf16) + (1 if nj == 1 else 2) * _nbytes((d, tn), bf16)
           + 2 * _nbytes((tm, tn), out_dtype) + 4 * _nbytes((tm, chunk), f32)
           + sum(2 * _nbytes(s.block_shape, f32) for s in aux_specs))
    body = functools.partial(_proj_body, epilogue=epilogue, chunk=chunk, n_aux=len(aux))
    return pl.pallas_call(
        body, out_shape=SDS((m, n), out_dtype), grid=(m // tm, nj),
        in_specs=[pl.BlockSpec((tm, d), lambda i, j: (i, 0)), w_spec, *aux_specs],
        out_specs=pl.BlockSpec((tm, tn), lambda i, j: (i, j)),
        compiler_params=_params(("parallel", "arbitrary"), est), name=name)(h, w, *aux)


def _xbc_body(h_ref, w_ref, cw_ref, cb_ref, o_ref, tail_ref, halo_scr, xbuf, *, chunk, tiles_per_batch):
    tm = h_ref.shape[0]
    halo = SSD_CONV - 1

    @pl.when(lax.rem(pl.program_id(0), tiles_per_batch) == 0)
    def _():
        halo_scr[...] = jnp.zeros_like(halo_scr)

    h = h_ref[...]
    for c0 in range(0, w_ref.shape[1], chunk):
        cs = slice(c0, c0 + chunk)
        acc = jnp.dot(h, w_ref[:, cs], preferred_element_type=f32)
        last = acc[tm - SUBLANES:, :]
        tail_ref[:, cs] = last
        xbuf[0:SUBLANES, :] = halo_scr[:, cs]
        xbuf[SUBLANES:, :] = acc
        halo_scr[:, cs] = last
        conv = cb_ref[:, cs] + cw_ref[halo:halo + 1, cs] * acc
        for k in range(halo):
            conv = conv + cw_ref[k:k + 1, cs] * xbuf[SUBLANES - halo + k:SUBLANES - halo + k + tm, :]
        o_ref[:, cs] = _silu(conv).astype(o_ref.dtype)


def _xbc_proj(h, w, conv_w, conv_b, tm, tiles_per_batch):
    m, d = h.shape
    n = w.shape[1]
    chunk = 512
    nb = m // (tm * tiles_per_batch)
    est = (2 * _nbytes((tm, d), bf16) + _nbytes((d, n), bf16) + 2 * _nbytes((tm, n), bf16)
           + 8 * _nbytes((tm, chunk), f32))
    body = functools.partial(_xbc_body, chunk=chunk, tiles_per_batch=tiles_per_batch)
    return pl.pallas_call(
        body, out_shape=[SDS((m, n), bf16), SDS((nb * SUBLANES, n), f32)], grid=(m // tm,),
        in_specs=[pl.BlockSpec((tm, d), lambda i: (i, 0)), _resident((d, n)), _resident(conv_w.shape),
                  _resident(conv_b.shape)],
        out_specs=[pl.BlockSpec((tm, n), lambda i: (i, 0)),
                   pl.BlockSpec((SUBLANES, n), lambda i: (i // tiles_per_batch, 0))],
        scratch_shapes=[pltpu.VMEM((SUBLANES, n), f32), pltpu.VMEM((SUBLANES + tm, chunk), f32)],
        compiler_params=_params(("arbitrary",), est), name="xbc_proj")(h, w, conv_w, conv_b)


def _swa_prompt_body(sink_ref, q_ref, k_ref, v_ref, o_ref, bias_scr, *, nblk):
    blk = SWA_BLOCK

    @pl.when(pl.program_id(0) == 0)
    def _():
        row = lax.broadcasted_iota(jnp.int32, (blk, 2 * blk), 0)
        col = lax.broadcasted_iota(jnp.int32, (blk, 2 * blk), 1)
        dist = row + blk - col
        allowed = (dist >= 0) & (dist <= WINDOW)
        distf = dist.astype(f32)
        for hh in range(SWA_HEADS):
            slope = 2.0 ** (-8.0 * (hh + 1) / SWA_HEADS) * LOG2_E
            bias = jnp.where(allowed, -slope * distf, MASKED)
            bias_scr[1, hh] = bias
            bias_scr[0, hh] = jnp.where(col >= blk, bias, MASKED)

    lane = lax.broadcasted_iota(jnp.int32, (blk, LANES), 1)
    low_half = lane < SWA_HEAD_DIM
    zero = jnp.zeros((blk, LANES), bf16)

    def block(j, carry):
        r0 = pl.multiple_of(j * blk, blk)
        rp = pl.multiple_of(jnp.maximum(j - 1, 0) * blk, blk)
        first = jnp.minimum(j, 1)
        for g in range(SWA_KV_HEADS):
            ks = slice(g * LANES, (g + 1) * LANES)
            vs = slice(2 * g * LANES, 2 * (g + 1) * LANES)
            kcat = jnp.concatenate([k_ref[pl.ds(rp, blk), ks], k_ref[pl.ds(r0, blk), ks]], axis=0)
            vcat = jnp.concatenate([v_ref[pl.ds(rp, blk), vs], v_ref[pl.ds(r0, blk), vs]], axis=0)
            for pr in range(SWA_GROUP // 2):
                c0 = (g * SWA_GROUP + 2 * pr) * SWA_HEAD_DIM
                q2 = q_ref[pl.ds(r0, blk), c0:c0 + LANES]
                res = []
                for half in range(2):
                    r = 2 * pr + half
                    qm = jnp.where(low_half, q2, zero) if half == 0 else jnp.where(low_half, zero, q2)
                    s = lax.dot_general(qm, kcat, NT_DIMS, preferred_element_type=f32)
                    s = s + bias_scr[first, g * SWA_GROUP + r]
                    sink = sink_ref[g * SWA_GROUP + r]
                    mx = jnp.maximum(jnp.max(s, axis=-1, keepdims=True), sink)
                    p = jnp.exp2(s - mx)
                    o = jnp.dot(p.astype(bf16), vcat, preferred_element_type=f32)
                    es = jnp.exp2(sink - mx)
                    lo, hi = o[:, :LANES], o[:, LANES:]
                    res.append(lo / (hi + es) if half == 0 else hi / (lo + es))
                o_ref[pl.ds(r0, blk), c0:c0 + LANES] = jnp.where(low_half, res[0], res[1]).astype(o_ref.dtype)
        return carry

    lax.fori_loop(0, nblk, block, 0)


def _swa_prompt(q, kd, v4, sinks_log2, nbatch, t):
    nq, nk, nv = q.shape[1], kd.shape[1], v4.shape[1]
    bias_shape = (2, SWA_HEADS, SWA_BLOCK, 2 * SWA_BLOCK)
    est = (2 * (2 * _nbytes((t, nq), bf16) + _nbytes((t, nk), bf16) + _nbytes((t, nv), bf16)) + _nbytes(bias_shape, f32)
           + 32 * _nbytes((SWA_BLOCK, 2 * SWA_BLOCK), f32))
    body = functools.partial(_swa_prompt_body, nblk=t // SWA_BLOCK)
    rows = lambda width: pl.BlockSpec((t, width), lambda b: (b, 0))
    return pl.pallas_call(
        body, out_shape=SDS(q.shape, bf16), grid=(nbatch,),
        in_specs=[pl.BlockSpec(memory_space=pltpu.SMEM), rows(nq), rows(nk), rows(nv)],
        out_specs=rows(nq), scratch_shapes=[pltpu.VMEM(bias_shape, f32)],
        compiler_params=_params(("arbitrary",), est), name="swa_prompt")(sinks_log2, q, kd, v4)


def _swa_decode_body(q_ref, kn_ref, vn_ref, ck_ref, cv_ref, slope_ref, sink_ref, o_ref, ok_ref, ov_ref, *, bb, w_buf):
    flat = w_buf * SWA_KV_HEADS
    col = lax.broadcasted_iota(jnp.int32, (SWA_HEADS, flat), 1)
    row = lax.broadcasted_iota(jnp.int32, (SWA_HEADS, flat), 0)
    own = (col % SWA_KV_HEADS) == (row // SWA_GROUP)
    dist = (w_buf - col // SWA_KV_HEADS).astype(f32)
    bias = jnp.where(own, -slope_ref[:, 0:1] * dist, MASKED)
    sink = sink_ref[:, 0:1]
    row_kv_d = lax.broadcasted_iota(jnp.int32, (SWA_HEADS, SWA_HEAD_DIM), 0) // SWA_GROUP

    def own_rows(per_kv):
        out = jnp.zeros((SWA_HEADS, SWA_HEAD_DIM), f32)
        for g in range(SWA_KV_HEADS):
            out = jnp.where(row_kv_d == g, per_kv[g:g + 1, :], out)
        return out

    for b in range(bb):
        q = q_ref[b].astype(bf16)
        kn, vn = kn_ref[b], vn_ref[b]
        k = ck_ref[b].reshape(flat, SWA_HEAD_DIM).astype(bf16)
        v = cv_ref[b].reshape(flat, SWA_HEAD_DIM).astype(bf16)
        s = lax.dot_general(q, k, NT_DIMS, preferred_element_type=f32) + bias
        sn = jnp.sum(q.astype(f32) * own_rows(kn).astype(bf16).astype(f32), axis=-1, keepdims=True)
        mx = jnp.maximum(jnp.maximum(jnp.max(s, axis=-1, keepdims=True), sn), sink)
        p = jnp.exp(s - mx)
        pn = jnp.exp(sn - mx)
        den = jnp.sum(p, axis=-1, keepdims=True) + pn + jnp.exp(sink - mx)
        o = jnp.dot(p.astype(bf16), v, preferred_element_type=f32)
        o = o + pn.astype(bf16).astype(f32) * own_rows(vn).astype(bf16).astype(f32)
        o_ref[b] = (o / den).astype(o_ref.dtype)
        ok_ref[b, 0:w_buf - 1] = ck_ref[b, 1:w_buf]
        ok_ref[b, w_buf - 1] = kn
        ov_ref[b, 0:w_buf - 1] = cv_ref[b, 1:w_buf]
        ov_ref[b, w_buf - 1] = vn


def _swa_decode(q, k_new, v_new, cache_k, cache_v, slopes, sinks, bb):
    nb, w_buf, kvh, hd = cache_k.shape
    padded = _nbytes((bb, w_buf, SUBLANES, LANES), f32)
    est = 2 * 4 * padded + (2 << 20)
    body = functools.partial(_swa_decode_body, bb=bb, w_buf=w_buf)
    cache_spec = pl.BlockSpec((bb, w_buf, kvh, hd), lambda i: (i, 0, 0, 0))
    new_spec = pl.BlockSpec((bb, kvh, hd), lambda i: (i, 0, 0))
    q_spec = pl.BlockSpec((bb, SWA_HEADS, hd), lambda i: (i, 0, 0))
    return pl.pallas_call(
        body, out_shape=[SDS((nb, SWA_HEADS, hd), bf16), SDS(cache_k.shape, cache_k.dtype), SDS(cache_v.shape, cache_v.dtype)],
        grid=(nb // bb,),
        in_specs=[q_spec, new_spec, new_spec, cache_spec, cache_spec,
                  _resident((SWA_HEADS, LANES)), _resident((SWA_HEADS, LANES))],
        out_specs=[q_spec, cache_spec, cache_spec],
        compiler_params=_params(("parallel",), est), name="swa_decode")(q, k_new, v_new, cache_k, cache_v, slopes, sinks)


def _mem_decode_body(q_ref, k_ref, v_ref, o_ref, *, bb):
    rows = q_ref.shape[1]
    flat = k_ref.shape[1] * MEM_HEADS
    col = lax.broadcasted_iota(jnp.int32, (rows, flat), 1)
    row = lax.broadcasted_iota(jnp.int32, (rows, flat), 0)
    own = (col % MEM_HEADS) == (row % MEM_HEADS)
    for b in range(bb):
        k = k_ref[b].reshape(flat, MEM_HEAD_DIM).astype(bf16)
        v = v_ref[b].reshape(flat, MEM_HEAD_DIM).astype(bf16)
        s = lax.dot_general(q_ref[b], k, NT_DIMS, preferred_element_type=f32)
        s = jnp.where(own, s, MASKED)
        p = jnp.exp(s - jnp.max(s, axis=-1, keepdims=True))
        den = jnp.sum(p, axis=-1, keepdims=True)
        o = jnp.dot(p.astype(bf16), v, preferred_element_type=f32)
        o_ref[b] = (o / den).astype(o_ref.dtype)


def _mem_decode(q, k, v, bb):
    nb, rows, hd = q.shape
    kv_block = (bb,) + k.shape[1:]
    est = 2 * (2 * _nbytes(kv_block, f32) + 2 * _nbytes((bb, rows, hd), bf16)) + 3 * _nbytes(kv_block, bf16) // bb + (4 << 20)
    body = functools.partial(_mem_decode_body, bb=bb)
    q_spec = pl.BlockSpec((bb, rows, hd), lambda i: (i, 0, 0))
    kv_spec = pl.BlockSpec(kv_block, lambda i: (i, 0, 0, 0))
    return pl.pallas_call(
        body, out_shape=SDS(q.shape, bf16), grid=(nb // bb,), in_specs=[q_spec, kv_spec, kv_spec], out_specs=q_spec,
        compiler_params=_params(("parallel",), est), name="mem_decode")(q, k, v)


def _mem_attn_body(q_ref, k_ref, v_ref, o_ref, *, bb):
    for b in range(bb):
        for h in range(MEM_HEADS):
            hs = slice(h * MEM_HEAD_DIM, (h + 1) * MEM_HEAD_DIM)
            q = q_ref[b, :, hs]
            k = k_ref[b, :, hs].astype(bf16)
            v = v_ref[b, :, hs].astype(bf16)
            s = lax.dot_general(q, k, NT_DIMS, preferred_element_type=f32)
            p = jnp.exp(s - jnp.max(s, axis=-1, keepdims=True))
            den = jnp.sum(p, axis=-1, keepdims=True)
            o = jnp.dot(p.astype(bf16), v, preferred_element_type=f32)
            o_ref[b, :, hs] = (o / den).astype(o_ref.dtype)


def _mem_attn(q, k, v, bb, tq):
    nb, t, md = q.shape
    mt = k.shape[1]
    kv_spec = pl.BlockSpec((bb, mt, md), lambda i, j: (i, 0, 0))
    est = 2 * (2 * _nbytes((bb, tq, md), bf16) + 2 * _nbytes((bb, mt, md), f32)) + 4 * _nbytes((tq, mt), f32) + (2 << 20)
    body = functools.partial(_mem_attn_body, bb=bb)
    return pl.pallas_call(
        body, out_shape=SDS((nb, t, md), bf16), grid=(nb // bb, t // tq),
        in_specs=[pl.BlockSpec((bb, tq, md), lambda i, j: (i, j, 0)), kv_spec, kv_spec],
        out_specs=pl.BlockSpec((bb, tq, md), lambda i, j: (i, j, 0)),
        compiler_params=_params(("parallel", "arbitrary"), est), name="mem_attn")(q, k, v)


def _ssd_gate_norm(y, z, gain):
    y = y * z
    gsz = y.shape[1] // SSD_GROUPS
    outs = []
    for g in range(SSD_GROUPS):
        yg = y[:, g * gsz:(g + 1) * gsz]
        ms = jnp.mean(yg * yg, axis=-1, keepdims=True)
        outs.append(yg * lax.rsqrt(ms + EPS) * gain[:, g * gsz:(g + 1) * gsz])
    return jnp.concatenate(outs, axis=-1)


def _ssd_prompt_body(xc_ref, z_ref, dt_ref, dtb_ref, alog_ref, dsk_ref, gn_ref, rexp_ref, ltri_ref,
                     y_ref, hout_ref, h_scr):
    c = pl.program_id(1)

    @pl.when(c == 0)
    def _():
        h_scr[...] = jnp.zeros_like(h_scr)

    for r0 in range(0, z_ref.shape[0], SSD_CHUNK):
        _ssd_chunk(xc_ref, z_ref, dt_ref, dtb_ref, alog_ref, dsk_ref, gn_ref, rexp_ref, ltri_ref, y_ref, h_scr, r0)

    @pl.when(c == pl.num_programs(1) - 1)
    def _():
        hout_ref[0] = h_scr[...].T


def _ssd_chunk(xc_ref, z_ref, dt_ref, dtb_ref, alog_ref, dsk_ref, gn_ref, rexp_ref, ltri_ref, y_ref, h_scr, r0):
    chunk = SSD_CHUNK
    di = z_ref.shape[1]
    gn_w = SSD_GROUPS * SSD_D_STATE
    gw = di // SSD_GROUPS
    rs = slice(r0, r0 + chunk)

    xs = xc_ref[rs, :di].astype(f32)
    bm = xc_ref[rs, di:di + gn_w]
    cm = xc_ref[rs, di + gn_w:]

    dt = _softplus(dt_ref[rs, :] + dtb_ref[...])
    da = dt * (-jnp.exp(alog_ref[...]))
    acs = jnp.dot(ltri_ref[...], da, precision=lax.Precision.HIGHEST, preferred_element_type=f32)
    acs_t = acs.T
    dt_t = dt.T
    rexp = rexp_ref[...]
    eacs_e = _split_dot(jnp.exp(acs), rexp)
    xw = (xs * _split_dot(jnp.exp(acs[chunk - 1:chunk, :] - acs) * dt, rexp)).astype(bf16)

    row = lax.broadcasted_iota(jnp.int32, (chunk, chunk), 0)
    col = lax.broadcasted_iota(jnp.int32, (chunk, chunk), 1)
    causal = row >= col
    low_half = col < SSD_HEAD_DIM
    zero = jnp.zeros((chunk, LANES), bf16)
    heads_per_group = SSD_HEADS // SSD_GROUPS
    ys = []
    for g in range(SSD_GROUPS):
        ns = slice(g * SSD_D_STATE, (g + 1) * SSD_D_STATE)
        ls = slice(g * gw, (g + 1) * gw)
        cb = lax.dot_general(cm[:, ns], bm[:, ns], NT_DIMS, preferred_element_type=f32)
        hg = h_scr[:, ls]
        y_off = jnp.dot(cm[:, ns], hg.astype(bf16), preferred_element_type=f32) * eacs_e[:, ls]
        y_diag = []
        for pr in range(heads_per_group // 2):
            h0 = g * heads_per_group + 2 * pr
            xp = xc_ref[rs, h0 * SSD_HEAD_DIM:h0 * SSD_HEAD_DIM + LANES]
            ws = []
            for hh in (h0, h0 + 1):
                diff = acs[:, hh:hh + 1] - acs_t[hh:hh + 1, :]
                ws.append((cb * jnp.exp(jnp.where(causal, diff, MASKED)) * dt_t[hh:hh + 1, :]).astype(bf16))
            x2 = jnp.concatenate([jnp.where(low_half, xp, zero), jnp.where(low_half, zero, xp)], axis=0)
            y_diag.append(jnp.dot(jnp.concatenate(ws, axis=1), x2, preferred_element_type=f32))
        st = lax.dot_general(bm[:, ns], xw[:, ls], TN_DIMS, preferred_element_type=f32)
        h_scr[:, ls] = eacs_e[chunk - 1:chunk, ls] * hg + st
        ys.append(jnp.concatenate(y_diag, axis=-1) + y_off)
    y = jnp.concatenate(ys, axis=-1) + dsk_ref[...] * xs
    y_ref[rs, :] = _ssd_gate_norm(y, z_ref[rs, :].astype(f32), gn_ref[...]).astype(y_ref.dtype)


def _ssd_prompt(xc, z, dt, ssd_w, nbatch, t, chunks_per_step):
    _, _, dtb, alog, dsk, gn, rexp, ltri = ssd_w
    ch, di = xc.shape[1], z.shape[1]
    tr = chunks_per_step * SSD_CHUNK
    nstep = t // tr
    rows = lambda b, c: (b * nstep + c, 0)
    est = (2 * (_nbytes((tr, ch), bf16) + 2 * _nbytes((tr, di), bf16) + _nbytes((di, SSD_D_STATE), f32))
           + _nbytes((SSD_D_STATE, di), f32) + 24 * _nbytes((SSD_CHUNK, ch), f32))
    return pl.pallas_call(
        _ssd_prompt_body,
        out_shape=[SDS((nbatch * t, di), bf16), SDS((nbatch, di, SSD_D_STATE), f32)],
        grid=(nbatch, nstep),
        in_specs=[pl.BlockSpec((tr, ch), rows), pl.BlockSpec((tr, di), rows),
                  pl.BlockSpec((tr, LANES), rows),
                  _resident(dtb.shape), _resident(alog.shape),
                  _resident(dsk.shape), _resident(gn.shape), _resident(rexp.shape), _resident(ltri.shape)],
        out_specs=[pl.BlockSpec((tr, di), rows), pl.BlockSpec((1, di, SSD_D_STATE), lambda b, c: (b, 0, 0))],
        scratch_shapes=[pltpu.VMEM((SSD_D_STATE, di), f32)],
        compiler_params=_params(("parallel", "arbitrary"), est), name="ssd_prompt")(xc, z, dt, dtb, alog, dsk, gn, rexp, ltri)


def _ssd_step_body(xbc_ref, z_ref, dt_ref, cs_ref, h_ref, cw_ref, cb_ref, dtb_ref, alog_ref, dsk_ref, gn_ref, rexp_ref,
                   y_ref, cso_ref, ho_ref, xdt_scr, da_scr, bm_scr, cm_scr, y_scr, *, bb):
    ch = xbc_ref.shape[1]
    di = z_ref.shape[1]
    gn_w = SSD_GROUPS * SSD_D_STATE
    gw = di // SSD_GROUPS
    halo = SSD_CONV - 1
    xr = xbc_ref[...]
    conv = cb_ref[...] + cw_ref[halo:halo + 1, :] * xr
    for k in range(halo):
        conv = conv + cw_ref[k:k + 1, :] * cs_ref[:, k * ch:(k + 1) * ch]
    for k in range(1, halo):
        cso_ref[:, (k - 1) * ch:k * ch] = cs_ref[:, k * ch:(k + 1) * ch]
    cso_ref[:, (halo - 1) * ch:] = xr
    xc = _silu(conv)
    xs = xc[:, :di]
    dt = _softplus(dt_ref[...] + dtb_ref[...])
    rexp = rexp_ref[...]
    xdt_scr[...] = xs * _split_dot(dt, rexp)
    da_scr[...] = _split_dot(jnp.exp(dt * (-jnp.exp(alog_ref[...]))), rexp)
    bm_scr[...] = xc[:, di:di + gn_w]
    cm_scr[...] = xc[:, di + gn_w:]

    rows = 2 * SUBLANES
    rowi = lax.broadcasted_iota(jnp.int32, (rows, di), 0)
    grp = lax.broadcasted_iota(jnp.int32, (rows, di), 1) // gw
    rown = lax.broadcasted_iota(jnp.int32, (rows, SSD_D_STATE), 0)
    ones_rows = jnp.where((rown == SSD_GROUPS) | (rown == SSD_GROUPS + 1), 1.0, 0.0).astype(f32)

    def per_row(b, carry):
        xrow = xdt_scr[pl.ds(b, 1), :]
        drow = da_scr[pl.ds(b, 1), :]
        d_hi = drow.astype(bf16).astype(f32)
        lhs = jnp.where(rowi == grp, xrow, jnp.where(rowi == SSD_GROUPS, d_hi, jnp.where(rowi == SSD_GROUPS + 1, drow - d_hi, 0.0)))
        brow = bm_scr[pl.ds(b, 1), :]
        crow = cm_scr[pl.ds(b, 1), :]
        rhs_b = jnp.zeros((rows, SSD_D_STATE), f32)
        c_rows = jnp.zeros((rows, SSD_D_STATE), f32)
        for g in range(SSD_GROUPS):
            ns = slice(g * SSD_D_STATE, (g + 1) * SSD_D_STATE)
            rhs_b = jnp.where(rown == g, brow[:, ns], rhs_b)
            c_rows = jnp.where(rown == g, crow[:, ns], c_rows)
        rhs = jnp.concatenate([rhs_b, ones_rows], axis=1).astype(bf16)
        sd = lax.dot_general(lhs.astype(bf16), rhs, TN_DIMS, preferred_element_type=f32)
        hn = sd[:, SSD_D_STATE:] * h_ref[b] + sd[:, :SSD_D_STATE]
        ho_ref[b] = hn
        y8 = lax.dot_general(c_rows.astype(bf16), hn.astype(bf16), NT_DIMS, preferred_element_type=f32)
        y_scr[pl.ds(b, 1), :] = jnp.sum(jnp.where(rowi == grp, y8, 0.0), axis=0, keepdims=True)
        return carry

    lax.fori_loop(0, bb, per_row, 0)
    y = y_scr[...] + dsk_ref[...] * xs
    y_ref[...] = _ssd_gate_norm(y, _silu(z_ref[...]), gn_ref[...]).astype(y_ref.dtype)


def _ssd_step(xbc, z, dt, conv_state, h0, ssd_w, bb):
    cw, cb, dtb, alog, dsk, gn, rexp, _ = ssd_w
    nb, ch = xbc.shape
    di = z.shape[1]
    gn_w = SSD_GROUPS * SSD_D_STATE
    halo = SSD_CONV - 1
    est = (2 * (2 * _nbytes((bb, di, SSD_D_STATE), f32) + 2 * _nbytes((bb, halo * ch), f32) + 4 * _nbytes((bb, ch), f32))
           + 8 * _nbytes((di, 2 * SSD_D_STATE), f32))
    body = functools.partial(_ssd_step_body, bb=bb)
    r2 = lambda i: (i, 0)
    return pl.pallas_call(
        body,
        out_shape=[SDS((nb, di), bf16), SDS((nb, halo * ch), f32), SDS((nb, di, SSD_D_STATE), f32)],
        grid=(nb // bb,),
        in_specs=[pl.BlockSpec((bb, ch), r2), pl.BlockSpec((bb, di), r2), pl.BlockSpec((bb, LANES), r2),
                  pl.BlockSpec((bb, halo * ch), r2), pl.BlockSpec((bb, di, SSD_D_STATE), lambda i: (i, 0, 0)),
                  _resident(cw.shape), _resident(cb.shape), _resident(dtb.shape), _resident(alog.shape),
                  _resident(dsk.shape), _resident(gn.shape), _resident(rexp.shape)],
        out_specs=[pl.BlockSpec((bb, di), r2), pl.BlockSpec((bb, halo * ch), r2),
                   pl.BlockSpec((bb, di, SSD_D_STATE), lambda i: (i, 0, 0))],
        scratch_shapes=[pltpu.VMEM((bb, di), f32), pltpu.VMEM((bb, di), f32), pltpu.VMEM((bb, gn_w), f32),
                        pltpu.VMEM((bb, gn_w), f32), pltpu.VMEM((bb, di), f32)],
        compiler_params=_params(("parallel",), est), name="ssd_step")(xbc, z, dt, conv_state, h0, cw, cb, dtb, alog, dsk, gn, rexp)


def _merge_body(a_ref, s_ref, m_ref, g0_ref, g1_ref, g2_ref, wa_ref, ws_ref, wm_ref, o_ref, *, chunk):
    a, s, m = a_ref[...], s_ref[...], m_ref[...]
    for c0 in range(0, o_ref.shape[1], chunk):
        cs = slice(c0, c0 + chunk)
        acc = g0_ref[:, cs].astype(f32) * jnp.dot(a, wa_ref[:, cs], preferred_element_type=f32)
        acc = acc + g1_ref[:, cs].astype(f32) * jnp.dot(s, ws_ref[:, cs], preferred_element_type=f32)
        acc = acc + g2_ref[:, cs].astype(f32) * jnp.dot(m, wm_ref[:, cs], preferred_element_type=f32)
        o_ref[:, cs] = acc.astype(o_ref.dtype)


def _merge(a, s, mo, gates, wa, ws, wm, tm):
    m, d = s.shape[0], wa.shape[1]
    chunk = 512
    est = (2 * (_nbytes((tm, a.shape[1]), bf16) + _nbytes((tm, s.shape[1]), bf16) + _nbytes((tm, mo.shape[1]), bf16)
                + 4 * _nbytes((tm, d), bf16))
           + _nbytes(wa.shape, bf16) + _nbytes(ws.shape, bf16) + _nbytes(wm.shape, bf16) + 6 * _nbytes((tm, chunk), f32))
    body = functools.partial(_merge_body, chunk=chunk)
    return pl.pallas_call(
        body, out_shape=SDS((m, d), bf16), grid=(m // tm,),
        in_specs=[pl.BlockSpec((tm, a.shape[1]), lambda i: (i, 0)), pl.BlockSpec((tm, s.shape[1]), lambda i: (i, 0)),
                  pl.BlockSpec((tm, mo.shape[1]), lambda i: (i, 0)),
                  pl.BlockSpec((tm, d), lambda i: (i, 0)), pl.BlockSpec((tm, d), lambda i: (i, 1)),
                  pl.BlockSpec((tm, d), lambda i: (i, 2)),
                  _resident(wa.shape), _resident(ws.shape), _resident(wm.shape)],
        out_specs=pl.BlockSpec((tm, d), lambda i: (i, 0)),
        compiler_params=_params(("parallel",), est), name="merge")(a, s, mo, gates, gates, gates, wa, ws, wm)


def _out_proj_body(m_ref, w_ref, x_ref, g_ref, x1_ref, h2_ref, *, chunk):
    m = m_ref[...]
    for c0 in range(0, x1_ref.shape[1], chunk):
        cs = slice(c0, c0 + chunk)
        x1_ref[:, cs] = x_ref[:, cs] + jnp.dot(m, w_ref[:, cs], preferred_element_type=f32)
    x1 = x1_ref[...]
    ms = jnp.mean(x1 * x1, axis=-1, keepdims=True)
    h2_ref[...] = (x1 * lax.rsqrt(ms + EPS) * g_ref[...]).astype(h2_ref.dtype)


def _out_proj(merged, w, x, gain, tm):
    m, d = x.shape
    rows = pl.BlockSpec((tm, d), lambda i: (i, 0))
    est = 2 * (_nbytes((tm, d), bf16) + 2 * _nbytes((tm, d), f32) + _nbytes((tm, d), bf16)) + _nbytes(w.shape, bf16) + (4 << 20)
    body = functools.partial(_out_proj_body, chunk=512)
    return pl.pallas_call(
        body, out_shape=[SDS((m, d), f32), SDS((m, d), bf16)], grid=(m // tm,),
        in_specs=[rows, _resident(w.shape), rows, _resident((1, d))], out_specs=[rows, rows],
        compiler_params=_params(("parallel",), est), name="out_proj")(merged, w, x, gain.reshape(1, d))


def _ffn_act_body(h_ref, wg_ref, wu_ref, a_ref, *, chunk):
    h = h_ref[...]
    for s0 in range(0, a_ref.shape[1], chunk):
        cs = slice(s0, s0 + chunk)
        gate = jnp.dot(h, wg_ref[:, cs], preferred_element_type=f32)
        up = jnp.dot(h, wu_ref[:, cs], preferred_element_type=f32)
        a_ref[:, cs] = (_silu(gate) * up).astype(a_ref.dtype)


def _ffn_act(h2, wg, wu, tm, tf):
    m, d = h2.shape
    dff = wg.shape[1]
    w_spec = pl.BlockSpec((d, tf), lambda c, i: (0, c), pipeline_mode=pl.Buffered(1))
    est = 2 * _nbytes((d, tf), bf16) + 2 * (_nbytes((tm, d), bf16) + _nbytes((tm, tf), bf16)) + (6 << 20)
    body = functools.partial(_ffn_act_body, chunk=256)
    return pl.pallas_call(
        body, out_shape=SDS((m, dff), bf16), grid=(dff // tf, m // tm),
        in_specs=[pl.BlockSpec((tm, d), lambda c, i: (i, 0)), w_spec, w_spec],
        out_specs=pl.BlockSpec((tm, tf), lambda c, i: (i, c)),
        compiler_params=_params(("arbitrary", "arbitrary"), est), name="ffn_act")(h2, wg, wu)


def _ffn_down_body(a_ref, w_ref, x_ref, o_ref, *, chunk):
    a = a_ref[...]
    for c0 in range(0, o_ref.shape[1], chunk):
        cs = slice(c0, c0 + chunk)
        o_ref[:, cs] = x_ref[:, cs] + jnp.dot(a, w_ref[:, cs], preferred_element_type=f32)


def _ffn_down(act, wd, x1, tm):
    m, d = x1.shape
    dff = act.shape[1]
    rows = pl.BlockSpec((tm, d), lambda i: (i, 0))
    est = 2 * (_nbytes((tm, dff), bf16) + 2 * _nbytes((tm, d), f32)) + _nbytes(wd.shape, bf16) + (2 << 20)
    body = functools.partial(_ffn_down_body, chunk=512)
    return pl.pallas_call(
        body, out_shape=SDS((m, d), f32), grid=(m // tm,),
        in_specs=[pl.BlockSpec((tm, dff), lambda i: (i, 0)), _resident(wd.shape), rows], out_specs=rows,
        compiler_params=_params(("parallel",), est), name="ffn_down")(act, wd, x1)


def _dup_heads(w):
    lead = w.shape[:-1]
    w = w.reshape(*lead, SWA_KV_HEADS, 1, SWA_HEAD_DIM)
    return jnp.broadcast_to(w, (*lead, SWA_KV_HEADS, 2, SWA_HEAD_DIM)).reshape(*lead, SWA_KV_HEADS * LANES)


def _pad_lanes(v, n=LANES):
    return jnp.pad(v, [(0, 0)] * (v.ndim - 1) + [(0, n - v.shape[-1])])


def _layer(xp, xs, cache_k, cache_v, cmem_k, cmem_v, state_ssm, state_conv, mem_prompt, w):
    bp, tp, d = xp.shape
    bs = xs.shape[0]
    w_buf = cache_k.shape[1]
    assert xs.shape[1] == 1 and w_buf == WINDOW and tp % SSD_CHUNK == 0 and bs % SUBLANES == 0
    q_dim = SWA_HEADS * SWA_HEAD_DIM
    kv_dim = SWA_KV_HEADS * SWA_HEAD_DIM
    di = SSD_HEADS * SSD_HEAD_DIM
    ch = di + 2 * SSD_GROUPS * SSD_D_STATE
    md = MEM_HEADS * MEM_HEAD_DIM
    o = 0
    w_in = w["w_in"].astype(bf16)
    wq, o = w_in[:, o:o + q_dim], o + q_dim
    wk, o = w_in[:, o:o + kv_dim], o + kv_dim
    wv, o = w_in[:, o:o + kv_dim], o + kv_dim
    wz, o = w_in[:, o:o + di], o + di
    wxbc, o = w_in[:, o:o + ch], o + ch
    wdt, o = w_in[:, o:o + SSD_HEADS], o + SSD_HEADS
    wqm, o = w_in[:, o:o + md], o + md
    wg = w_in[:, o:]
    wdt = _pad_lanes(wdt)

    q_scale = SWA_HEAD_DIM ** -0.5
    gq = jnp.tile(w["q_norm_swa"], SWA_HEADS) * q_scale
    gk = jnp.tile(w["k_norm_swa"], SWA_KV_HEADS)
    head_of = np.arange(2 * LANES) // SWA_HEAD_DIM
    blockdiag = jnp.asarray(head_of[:, None] == head_of[None, :], bf16)
    w_qkv_p = jnp.concatenate([wq, _dup_heads(wk), _dup_heads(wv), wdt], axis=1)
    w_qkv_s = jnp.concatenate([wq, wk, wv, wdt], axis=1)
    gqm = (jnp.tile(w["q_norm_mem"], MEM_HEADS) * MEM_HEAD_DIM ** -0.5).reshape(1, md)
    gkm = jnp.tile(w["k_norm_mem"], MEM_HEADS).reshape(1, md)
    wz_b, wxbc_b, wqm_b, wg_b = wz, wxbc, wqm, wg
    w_mem_k, w_mem_v = w["w_mem_kv"][:, :md].astype(bf16), w["w_mem_kv"][:, md:].astype(bf16)
    wa, ws, wm = w["w_up_swa"].astype(bf16), w["w_up_ssd"].astype(bf16), w["w_up_mem"].astype(bf16)
    w_out = w["w_out"].astype(bf16)
    w_gate, w_up, w_down = w["w_gate"].astype(bf16), w["w_up"].astype(bf16), w["w_down"].astype(bf16)

    head_of_lane = np.arange(di) // SSD_HEAD_DIM
    rexp = jnp.asarray(np.arange(LANES)[:, None] == head_of_lane[None, :], bf16)
    ltri = jnp.asarray(np.arange(SSD_CHUNK)[:, None] >= np.arange(SSD_CHUNK)[None, :], f32)
    ssd_w = (w["conv_w"], w["conv_b"].reshape(1, ch), _pad_lanes(w["dt_bias"].reshape(1, -1)),
             _pad_lanes(w["a_log"].reshape(1, -1)), jnp.repeat(w["d_skip"], SSD_HEAD_DIM).reshape(1, di),
             w["ssd_norm"].reshape(1, di), rexp, ltri)
    slopes = jnp.asarray(np.broadcast_to(np.exp2(-8.0 * np.arange(1, SWA_HEADS + 1) / SWA_HEADS)[:, None], (SWA_HEADS, LANES)), f32)
    sinks_b = jnp.broadcast_to(w["swa_sinks"].astype(f32)[:, None], (SWA_HEADS, LANES))

    dff = w_gate.shape[1]
    tf = dff // 2 if dff % (4 * LANES) == 0 else dff

    def finish(x2, a_out, s_out, m_out, gates, tm, tm_act):
        merged = _merge(a_out, s_out, m_out, gates, wa, ws, wm, tm)
        x1, h2 = _out_proj(merged, w_out, x2, w["norm_ffn"], tm)
        act = _ffn_act(h2, w_gate, w_up, tm_act, tf)
        return _ffn_down(act, w_down, x1, tm)

    mp = bp * tp
    x2 = xp.reshape(mp, d)
    tm = min(1024, tp)
    tpb = tp // tm
    tm_qkv = min(512, tp)
    g_mix = w["norm_mix"].reshape(1, d)
    h, q, kd, v4, dt, klast, vlast = _qkv_proj(
        x2, g_mix, w_qkv_p, (gq * LOG2_E).reshape(1, -1), _dup_heads(gk).reshape(1, -1), blockdiag,
        q_dim, 2 * kv_dim, 2 * kv_dim, tm_qkv, tp // tm_qkv, bf16, bf16, True)
    z = _proj(h, wz_b, _epi_silu, bf16, tm, di, name="z_proj")
    xc, xtail = _xbc_proj(h, wxbc_b, ssd_w[0], ssd_w[1], tm, tpb)
    qm = _proj(h, wqm_b, _epi_headnorm256, bf16, tm, md, aux=(gqm,), aux_specs=(_resident((1, md)),), name="qm_proj")
    gates = _proj(h, wg_b, _epi_sigmoid, bf16, tm, d, name="gate_proj")
    a_out = _swa_prompt(q, kd, v4, w["swa_sinks"].astype(f32) * LOG2_E, bp, tp)
    s_out, p_h = _ssd_prompt(xc, z, dt, ssd_w, bp, tp, 4 if tp % (4 * SSD_CHUNK) == 0 else 1)
    mt = mem_prompt.shape[1]
    hm = _rmsnorm(mem_prompt.reshape(bp * mt, d), w["norm_mem"], mt)
    mk = _proj(hm, w_mem_k, _epi_headnorm256, f32, mt, md, aux=(gkm,), aux_specs=(_resident((1, md)),), name="mem_k_proj")
    mv = _proj(hm, w_mem_v, _epi_plain, f32, mt, md, name="mem_v_proj")
    m_out = _mem_attn(qm.reshape(bp, tp, md), mk.reshape(bp, mt, md), mv.reshape(bp, mt, md), 1, tm)
    yp = finish(x2, a_out, s_out, m_out.reshape(mp, md), gates, min(512, tp), tm).reshape(bp, tp, d)

    undup = lambda t: t.reshape(bp, WINDOW, SWA_KV_HEADS, 2, SWA_HEAD_DIM)[:, :, :, 0, :]
    p_k, p_v = undup(klast), undup(vlast)
    p_mk = mk.reshape(bp, mt, MEM_HEADS, MEM_HEAD_DIM)
    p_mv = mv.reshape(bp, mt, MEM_HEADS, MEM_HEAD_DIM)
    p_h = p_h.reshape(bp, SSD_HEADS, SSD_HEAD_DIM, SSD_D_STATE)
    p_c = xtail.reshape(bp, SUBLANES, ch)[:, SUBLANES - (SSD_CONV - 1):, :]

    xs2 = xs.reshape(bs, d)
    hs, qs, ks, vs, dts = _qkv_proj(xs2, g_mix, w_qkv_s, gq.reshape(1, -1), gk.reshape(1, -1), blockdiag,
                                    q_dim, kv_dim, kv_dim, bs, 1, f32, f32, False)
    zs = _proj(hs, wz_b, _epi_plain, f32, bs, di, name="z_proj_s")
    xbcs = _proj(hs, wxbc_b, _epi_plain, f32, bs, ch, name="xbc_proj_s")
    qms = _proj(hs, wqm_b, _epi_headnorm256, bf16, bs, md, aux=(gqm,), aux_specs=(_resident((1, md)),), name="qm_proj_s")
    gates_s = _proj(hs, wg_b, _epi_sigmoid, bf16, bs, d, name="gate_proj_s")
    a_s, s_k, s_v = _swa_decode(qs.reshape(bs, SWA_HEADS, SWA_HEAD_DIM), ks.reshape(bs, SWA_KV_HEADS, SWA_HEAD_DIM),
                                vs.reshape(bs, SWA_KV_HEADS, SWA_HEAD_DIM), cache_k, cache_v, slopes, sinks_b, SUBLANES)
    halo = SSD_CONV - 1
    s_s, s_c, s_h = _ssd_step(xbcs, zs, dts, state_conv.reshape(bs, halo * ch), state_ssm.reshape(bs, di, SSD_D_STATE),
                              ssd_w, SUBLANES)
    qm_rows = jnp.tile(qms.reshape(bs, MEM_HEADS, MEM_HEAD_DIM), (1, 2 * SUBLANES // MEM_HEADS, 1))
    m_s = _mem_decode(qm_rows, cmem_k, cmem_v, 2)[:, :MEM_HEADS, :].reshape(bs, md)
    ys = finish(xs2, a_s.reshape(bs, q_dim), s_s, m_s, gates_s, bs, bs).reshape(bs, 1, d)

    s_h =s_h.reshape(bs, SSD_HEADS, SSD_HEAD_DIM, SSD_D_STATE)
    s_c = s_c.reshape(bs, halo, ch)
    return yp, ys, (p_k, p_v, p_mk, p_mv, p_h, p_c), (s_k, s_v, s_h, s_c)


_WEIGHT_NAMES = ("norm_mix", "w_in", "q_norm_swa", "k_norm_swa", "swa_sinks", "conv_w", "conv_b", "dt_bias", "a_log",
                 "d_skip", "ssd_norm", "norm_mem", "w_mem_kv", "q_norm_mem", "k_norm_mem", "w_up_swa", "w_up_ssd",
                 "w_up_mem", "w_out", "norm_ffn", "w_gate", "w_up", "w_down")


def kernel(x_prompt, x_sample, cache_swa_k, cache_swa_v, cache_mem_k, cache_mem_v, state_ssm, state_conv, mem_prompt, norm_mix, w_in, q_norm_swa, k_norm_swa, swa_sinks, conv_w, conv_b, dt_bias, a_log, d_skip, ssd_norm, norm_mem, w_mem_kv, q_norm_mem, k_norm_mem, w_up_swa, w_up_ssd, w_up_mem, w_out, norm_ffn, w_gate, w_up, w_down):
    weights = (norm_mix, w_in, q_norm_swa, k_norm_swa, swa_sinks, conv_w, conv_b, dt_bias, a_log, d_skip, ssd_norm,
               norm_mem, w_mem_kv, q_norm_mem, k_norm_mem, w_up_swa, w_up_ssd, w_up_mem, w_out, norm_ffn, w_gate, w_up, w_down)
    depth = w_in.shape[0]
    layer = (lambda a, l: a.reshape(a.shape[1:])) if depth == 1 else (lambda a, l: a[l])
    yp, ys = x_prompt, x_sample
    p_outs, s_outs = [], []
    for l in range(depth):
        w = {n: layer(a, l) for n, a in zip(_WEIGHT_NAMES, weights)}
        yp, ys, po, so = _layer(yp, ys, layer(cache_swa_k, l), layer(cache_swa_v, l), layer(cache_mem_k, l),
                                layer(cache_mem_v, l), layer(state_ssm, l), layer(state_conv, l), mem_prompt, w)
        p_outs.append(po)
        s_outs.append(so)
    stack = lambda outs, i: jnp.stack([o[i] for o in outs])
    return (yp, ys, *(stack(p_outs, i) for i in range(6)), *(stack(s_outs, i) for i in range(4)))
```

```python
import functools

import jax
import jax.numpy as jnp
import numpy as np
from jax import lax
from jax.experimental import pallas as pl
from jax.experimental.pallas import tpu as pltpu

f32 = jnp.float32
bf16 = jnp.bfloat16
SDS = jax.ShapeDtypeStruct

LANES = 128
SUBLANES = 8
VMEM_BYTES_V7X = 64 * 1024 * 1024
VMEM_HEADROOM = 8 * 1024 * 1024

EPS = 1e-6
SWA_HEADS = 16
SWA_KV_HEADS = 4
SWA_GROUP = SWA_HEADS // SWA_KV_HEADS
SWA_HEAD_DIM = 64
WINDOW = 128
SWA_BLOCK = 128
SSD_HEAD_DIM = 64
SSD_HEADS = 32
SSD_GROUPS = 4
SSD_D_STATE = 128
SSD_CONV = 4
SSD_CHUNK = 128
MEM_HEADS = 4
MEM_HEAD_DIM = 256
N_BRANCH = 3
MASKED = -1e30
LOG2_E = 1.4426950408889634

NT_DIMS = (((1,), (1,)), ((), ()))
TN_DIMS = (((0,), (0,)), ((), ()))


def _params(semantics, block_bytes):
    limit = min(int(block_bytes) + VMEM_HEADROOM, VMEM_BYTES_V7X - VMEM_HEADROOM)
    return pltpu.CompilerParams(dimension_semantics=semantics, vmem_limit_bytes=limit)


def _nbytes(shape, dtype):
    n = 1
    for s in shape:
        n *= s
    return n * jnp.dtype(dtype).itemsize


def _resident(shape):
    nd = len(shape)
    return pl.BlockSpec(shape, lambda *_: (0,) * nd, pipeline_mode=pl.Buffered(1))


def _sigmoid(x):
    return 0.5 + 0.5 * jnp.tanh(0.5 * x)


def _silu(x):
    hx = 0.5 * x
    return hx + hx * jnp.tanh(hx)


def _softplus(x):
    return jnp.maximum(x, 0.0) + jnp.log1p(jnp.exp(-jnp.abs(x)))


def _split_dot(v, m):
    hi = v.astype(bf16)
    lo = (v - hi.astype(f32)).astype(bf16)
    return jnp.dot(hi, m, preferred_element_type=f32) + jnp.dot(lo, m, preferred_element_type=f32)


def _headnorm64(acc, gain, blockdiag):
    outs = []
    wide = blockdiag.shape[0]
    for c in range(acc.shape[1] // wide):
        a = acc[:, c * wide:(c + 1) * wide]
        ss = jnp.dot((a * a).astype(bf16), blockdiag, preferred_element_type=f32)
        outs.append(a * lax.rsqrt(ss * (1.0 / SWA_HEAD_DIM) + EPS) * gain[:, c * wide:(c + 1) * wide])
    return outs[0] if len(outs) == 1 else jnp.concatenate(outs, axis=-1)


def _norm_body(x_ref, g_ref, o_ref):
    x = x_ref[...]
    ms = jnp.mean(x * x, axis=-1, keepdims=True)
    o_ref[...] = (x * lax.rsqrt(ms + EPS) * g_ref[...]).astype(o_ref.dtype)


def _rmsnorm(x, gain, tm):
    m, d = x.shape
    est = 2 * (_nbytes((tm, d), f32) + _nbytes((tm, d), bf16)) + _nbytes((tm, d), f32)
    return pl.pallas_call(
        _norm_body, out_shape=SDS((m, d), bf16), grid=(m // tm,),
        in_specs=[pl.BlockSpec((tm, d), lambda i: (i, 0)), _resident((1, d))],
        out_specs=pl.BlockSpec((tm, d), lambda i: (i, 0)),
        compiler_params=_params(("parallel",), est), name="rmsnorm")(x, gain.reshape(1, d))


def _qkv_body(x_ref, gx_ref, w_ref, gq_ref, gk_ref, bd_ref, h_ref, q_ref, k_ref, *rest, nq, nk, nv, chunk, prompt):
    v_ref, dt_ref, *last_refs = rest
    x = x_ref[...]
    h = (x * lax.rsqrt(jnp.mean(x * x, axis=-1, keepdims=True) + EPS) * gx_ref[...]).astype(bf16)
    h_ref[...] = h
    bd = bd_ref[...]
    tm = h.shape[0]
    for c0 in range(0, nq, chunk):
        acc = jnp.dot(h, w_ref[:, c0:c0 + chunk], preferred_element_type=f32)
        q_ref[:, c0:c0 + chunk] = _headnorm64(acc, gq_ref[:, c0:c0 + chunk], bd).astype(q_ref.dtype)
    for c0 in range(0, nk, chunk):
        w = min(chunk, nk - c0)
        acc = jnp.dot(h, w_ref[:, nq + c0:nq + c0 + w], preferred_element_type=f32)
        kn = _headnorm64(acc, gk_ref[:, c0:c0 + w], bd)
        k_ref[:, c0:c0 + w] = kn.astype(k_ref.dtype)
        if last_refs:
            last_refs[0][:, c0:c0 + w] = kn[tm - WINDOW:, :]
    for c0 in range(0, nv, chunk):
        w = min(chunk, nv - c0)
        acc = jnp.dot(h, w_ref[:, nq + nk + c0:nq + nk + c0 + w], preferred_element_type=f32)
        if prompt:
            low_half = lax.broadcasted_iota(jnp.int32, (tm, LANES), 1) < SWA_HEAD_DIM
            for t0 in range(0, w, LANES):
                a = acc[:, t0:t0 + LANES]
                v_ref[:, 2 * (c0 + t0):2 * (c0 + t0) + LANES] = jnp.where(low_half, a, 1.0).astype(v_ref.dtype)
                v_ref[:, 2 * (c0 + t0) + LANES:2 * (c0 + t0 + LANES)] = jnp.where(low_half, 1.0, a).astype(v_ref.dtype)
            last_refs[1][:, c0:c0 + w] = acc[tm - WINDOW:, :]
        else:
            v_ref[:, c0:c0 + w] = acc.astype(v_ref.dtype)
    dt_ref[...] = jnp.dot(h, w_ref[:, nq + nk + nv:], preferred_element_type=f32)


def _qkv_proj(x, gx, w, gq, gk, blockdiag, nq, nk, nv, tm, tiles_per_batch, q_dtype, kv_dtype, prompt):
    m, d = x.shape
    n = w.shape[1]
    chunk = 512
    rows = lambda width: pl.BlockSpec((tm, width), lambda i: (i, 0))
    v_width = 2 * nv if prompt else nv
    out_shape = [SDS((m, d), bf16), SDS((m, nq), q_dtype), SDS((m, nk), kv_dtype), SDS((m, v_width), kv_dtype),
                 SDS((m, LANES), f32)]
    out_specs = [rows(d), rows(nq), rows(nk), rows(v_width), rows(LANES)]
    if prompt:
        nb = m // (tm * tiles_per_batch)
        out_shape += [SDS((nb * WINDOW, nk), f32), SDS((nb * WINDOW, nv), f32)]
        out_specs += [pl.BlockSpec((WINDOW, nk), lambda i: (i // tiles_per_batch, 0)),
                      pl.BlockSpec((WINDOW, nv), lambda i: (i // tiles_per_batch, 0))]
    est = (2 * _nbytes((tm, d), f32) + 3 * _nbytes((tm, d), bf16) + _nbytes((d, n), bf16)
           + 2 * _nbytes((tm, nq + nk + 2 * nv), f32) + 4 * _nbytes((tm, chunk), f32))
    body = functools.partial(_qkv_body, nq=nq, nk=nk, nv=nv, chunk=chunk, prompt=prompt)
    return pl.pallas_call(
        body, out_shape=out_shape, grid=(m // tm,),
        in_specs=[pl.BlockSpec((tm, d), lambda i: (i, 0)), _resident((1, d)), _resident((d, n)), _resident((1, nq)),
                  _resident((1, nk)), _resident(blockdiag.shape)],
        out_specs=out_specs, compiler_params=_params(("arbitrary",), est), name="qkv_proj")(x, gx, w, gq, gk, blockdiag)


def _proj_body(h_ref, w_ref, *refs, epilogue, chunk, n_aux):
    aux, outs = refs[:n_aux], refs[n_aux:]
    h = h_ref[...]
    tn = w_ref.shape[1]
    for c0 in range(0, tn, chunk):
        acc = jnp.dot(h, w_ref[:, c0:c0 + chunk], preferred_element_type=f32)
        epilogue(acc, c0, chunk, aux, outs)


def _epi_plain(acc, c0, w, aux, outs):
    outs[0][:, c0:c0 + w] = acc.astype(outs[0].dtype)


def _epi_silu(acc, c0, w, aux, outs):
    outs[0][:, c0:c0 + w] = _silu(acc).astype(outs[0].dtype)


def _epi_sigmoid(acc, c0, w, aux, outs):
    outs[0][:, c0:c0 + w] = _sigmoid(acc).astype(outs[0].dtype)


def _epi_headnorm256(acc, c0, w, aux, outs):
    gain = aux[0]
    for h0 in range(0, w, MEM_HEAD_DIM):
        a = acc[:, h0:h0 + MEM_HEAD_DIM]
        ms = jnp.mean(a * a, axis=-1, keepdims=True)
        y = a * lax.rsqrt(ms + EPS) * gain[:, c0 + h0:c0 + h0 + MEM_HEAD_DIM]
        outs[0][:, c0 + h0:c0 + h0 + MEM_HEAD_DIM] = y.astype(outs[0].dtype)


def _proj(h, w, epilogue, out_dtype, tm, tn, aux=(), aux_specs=(), chunk=512, name="proj"):
    m, d = h.shape
    n = w.shape[1]
    nj = n // tn
    chunk = min(chunk, tn)
    w_spec = _resident((d, n)) if nj == 1 else pl.BlockSpec((d, tn), lambda i, j: (0, j))
    est = (2 * _nbytes((tm, d), bf16) + (1 if nj == 1 else 2) * _nbytes((d, tn), bf16)
           + 2 * _nbytes((tm, tn), out_dtype) + 4 * _nbytes((tm, chunk), f32)
           + sum(2 * _nbytes(s.block_shape, f32) for s in aux_specs))
    body = functools.partial(_proj_body, epilogue=epilogue, chunk=chunk, n_aux=len(aux))
    return pl.pallas_call(
        body, out_shape=SDS((m, n), out_dtype), grid=(m // tm, nj),
        in_specs=[pl.BlockSpec((tm, d), lambda i, j: (i, 0)), w_spec, *aux_specs],
        out_specs=pl.BlockSpec((tm, tn), lambda i, j: (i, j)),
        compiler_params=_params(("parallel", "arbitrary"), est), name=name)(h, w, *aux)


def _xbc_body(h_ref, w_ref, cw_ref, cb_ref, o_ref, tail_ref, halo_scr, xbuf, *, chunk, tiles_per_batch):
    tm = h_ref.shape[0]
    halo = SSD_CONV - 1

    @pl.when(lax.rem(pl.program_id(0), tiles_per_batch) == 0)
    def _():
        halo_scr[...] = jnp.zeros_like(halo_scr)

    h = h_ref[...]
    for c0 in range(0, w_ref.shape[1], chunk):
        cs = slice(c0, c0 + chunk)
        acc = jnp.dot(h, w_ref[:, cs], preferred_element_type=f32)
        last = acc[tm - SUBLANES:, :]
        tail_ref[:, cs] = last
        xbuf[0:SUBLANES, :] = halo_scr[:, cs]
        xbuf[SUBLANES:, :] = acc
        halo_scr[:, cs] = last
        conv = cb_ref[:, cs] + cw_ref[halo:halo + 1, cs] * acc
        for k in range(halo):
            conv = conv + cw_ref[k:k + 1, cs] * xbuf[SUBLANES - halo + k:SUBLANES - halo + k + tm, :]
        o_ref[:, cs] = _silu(conv).astype(o_ref.dtype)


def _xbc_proj(h, w, conv_w, conv_b, tm, tiles_per_batch):
    m, d = h.shape
    n = w.shape[1]
    chunk = 512
    nb = m // (tm * tiles_per_batch)
    est = (2 * _nbytes((tm, d), bf16) + _nbytes((d, n), bf16) + 2 * _nbytes((tm, n), bf16)
           + 8 * _nbytes((tm, chunk), f32))
    body = functools.partial(_xbc_body, chunk=chunk, tiles_per_batch=tiles_per_batch)
    return pl.pallas_call(
        body, out_shape=[SDS((m, n), bf16), SDS((nb * SUBLANES, n), f32)], grid=(m // tm,),
        in_specs=[pl.BlockSpec((tm, d), lambda i: (i, 0)), _resident((d, n)), _resident(conv_w.shape),
                  _resident(conv_b.shape)],
        out_specs=[pl.BlockSpec((tm, n), lambda i: (i, 0)),
                   pl.BlockSpec((SUBLANES, n), lambda i: (i // tiles_per_batch, 0))],
        scratch_shapes=[pltpu.VMEM((SUBLANES, n), f32), pltpu.VMEM((SUBLANES + tm, chunk), f32)],
        compiler_params=_params(("arbitrary",), est), name="xbc_proj")(h, w, conv_w, conv_b)


def _swa_prompt_body(sink_ref, q_ref, k_ref, v_ref, o_ref, bias_scr, *, nblk):
    blk = SWA_BLOCK

    @pl.when(pl.program_id(0) == 0)
    def _():
        row = lax.broadcasted_iota(jnp.int32, (blk, 2 * blk), 0)
        col = lax.broadcasted_iota(jnp.int32, (blk, 2 * blk), 1)
        dist = row + blk - col
        allowed = (dist >= 0) & (dist <= WINDOW)
        distf = dist.astype(f32)
        for hh in range(SWA_HEADS):
            slope = 2.0 ** (-8.0 * (hh + 1) / SWA_HEADS) * LOG2_E
            bias = jnp.where(allowed, -slope * distf, MASKED)
            bias_scr[1, hh] = bias
            bias_scr[0, hh] = jnp.where(col >= blk, bias, MASKED)

    lane = lax.broadcasted_iota(jnp.int32, (blk, LANES), 1)
    low_half = lane < SWA_HEAD_DIM
    zero = jnp.zeros((blk, LANES), bf16)

    def block(j, carry):
        r0 = pl.multiple_of(j * blk, blk)
        rp = pl.multiple_of(jnp.maximum(j - 1, 0) * blk, blk)
        first = jnp.minimum(j, 1)
        for g in range(SWA_KV_HEADS):
            ks = slice(g * LANES, (g + 1) * LANES)
            vs = slice(2 * g * LANES, 2 * (g + 1) * LANES)
            kcat = jnp.concatenate([k_ref[pl.ds(rp, blk), ks], k_ref[pl.ds(r0, blk), ks]], axis=0)
            vcat = jnp.concatenate([v_ref[pl.ds(rp, blk), vs], v_ref[pl.ds(r0, blk), vs]], axis=0)
            for pr in range(SWA_GROUP // 2):
                c0 = (g * SWA_GROUP + 2 * pr) * SWA_HEAD_DIM
                q2 = q_ref[pl.ds(r0, blk), c0:c0 + LANES]
                res = []
                for half in range(2):
                    r = 2 * pr + half
                    qm = jnp.where(low_half, q2, zero) if half == 0 else jnp.where(low_half, zero, q2)
                    s = lax.dot_general(qm, kcat, NT_DIMS, preferred_element_type=f32)
                    s = s + bias_scr[first, g * SWA_GROUP + r]
                    sink = sink_ref[g * SWA_GROUP + r]
                    mx = jnp.maximum(jnp.max(s, axis=-1, keepdims=True), sink)
                    p = jnp.exp2(s - mx)
                    o = jnp.dot(p.astype(bf16), vcat, preferred_element_type=f32)
                    es = jnp.exp2(sink - mx)
                    lo, hi = o[:, :LANES], o[:, LANES:]
                    res.append(lo / (hi + es) if half == 0 else hi / (lo + es))
                o_ref[pl.ds(r0, blk), c0:c0 + LANES] = jnp.where(low_half, res[0], res[1]).astype(o_ref.dtype)
        return carry

    lax.fori_loop(0, nblk, block, 0)


def _swa_prompt(q, kd, v4, sinks_log2, nbatch, t):
    nq, nk, nv = q.shape[1], kd.shape[1], v4.shape[1]
    bias_shape = (2, SWA_HEADS, SWA_BLOCK, 2 * SWA_BLOCK)
    est = (2 * (2 * _nbytes((t, nq), bf16) + _nbytes((t, nk), bf16) + _nbytes((t, nv), bf16)) + _nbytes(bias_shape, f32)
           + 32 * _nbytes((SWA_BLOCK, 2 * SWA_BLOCK), f32))
    body = functools.partial(_swa_prompt_body, nblk=t // SWA_BLOCK)
    rows = lambda width: pl.BlockSpec((t, width), lambda b: (b, 0))
    return pl.pallas_call(
        body, out_shape=SDS(q.shape, bf16), grid=(nbatch,),
        in_specs=[pl.BlockSpec(memory_space=pltpu.SMEM), rows(nq), rows(nk), rows(nv)],
        out_specs=rows(nq), scratch_shapes=[pltpu.VMEM(bias_shape, f32)],
        compiler_params=_params(("arbitrary",), est), name="swa_prompt")(sinks_log2, q, kd, v4)


def _swa_decode_body(q_ref, kn_ref, vn_ref, knt_ref, vnt_ref, ck_ref, cv_ref, slope_ref, sink_ref, o_ref, ok_ref, ov_ref,
                     *, bb, w_buf):
    feat = SWA_KV_HEADS * SWA_HEAD_DIM
    lane_kv = lax.broadcasted_iota(jnp.int32, (SWA_HEADS, feat), 1) // SWA_HEAD_DIM
    row_kv = lax.broadcasted_iota(jnp.int32, (SWA_HEADS, feat), 0) // SWA_GROUP
    own = lane_kv == row_kv
    tok = lax.broadcasted_iota(jnp.int32, (SWA_HEADS, w_buf), 1)
    bias = slope_ref[:, :w_buf] * (tok - w_buf).astype(f32)
    sink = sink_ref[:, 0:1]
    last_tok = lax.broadcasted_iota(jnp.int32, (feat, w_buf), 1) == w_buf - 1

    for b in range(bb):
        q = q_ref[b].astype(bf16)
        kn = kn_ref[b].astype(bf16).astype(f32)
        vn = vn_ref[b].astype(bf16).astype(f32)
        kt = ck_ref[b].reshape(feat, w_buf)
        vt = cv_ref[b].reshape(feat, w_buf)
        s = jnp.dot(q, kt.astype(bf16), preferred_element_type=f32) + bias
        sn = jnp.sum(q.astype(f32) * kn, axis=-1, keepdims=True)
        mx = jnp.maximum(jnp.maximum(jnp.max(s, axis=-1, keepdims=True), sn), sink)
        p = jnp.exp(s - mx)
        pn = jnp.exp(sn - mx)
        den = jnp.sum(p, axis=-1, keepdims=True) + pn + jnp.exp(sink - mx)
        o = lax.dot_general(p.astype(bf16), vt.astype(bf16), NT_DIMS, preferred_element_type=f32)
        o = o + pn.astype(bf16).astype(f32) * vn
        o_ref[b] = jnp.where(own, o / den, 0.0).astype(o_ref.dtype)
        ok_ref[b] = jnp.where(last_tok, knt_ref[0, :, b:b + 1], pltpu.roll(kt, w_buf - 1, 1)).reshape(ok_ref.shape[1:])
        ov_ref[b] = jnp.where(last_tok, vnt_ref[0, :, b:b + 1], pltpu.roll(vt, w_buf - 1, 1)).reshape(ov_ref.shape[1:])


def _swa_decode(q_blk, k_new, v_new, cache_kt, cache_vt, slopes, sinks, bb):
    nb, kvh, hd, w_buf = cache_kt.shape
    feat = kvh * hd
    est = 2 * (4 * _nbytes((bb, feat, w_buf), f32) + 2 * _nbytes((bb, SWA_HEADS, feat), f32)) + (4 << 20)
    body = functools.partial(_swa_decode_body, bb=bb, w_buf=w_buf)
    cache_spec = pl.BlockSpec((bb, kvh, hd, w_buf), lambda i: (i, 0, 0, 0))
    row_spec = pl.BlockSpec((bb, 1, feat), lambda i: (i, 0, 0))
    col_spec = pl.BlockSpec((1, feat, bb), lambda i: (i, 0, 0))
    q_spec = pl.BlockSpec((bb, SWA_HEADS, feat), lambda i: (i, 0, 0))
    as_cols = lambda t: jnp.transpose(t.reshape(nb // bb, bb, feat), (0, 2, 1))
    return pl.pallas_call(
        body, out_shape=[SDS((nb, SWA_HEADS, feat), f32), SDS(cache_kt.shape, cache_kt.dtype), SDS(cache_vt.shape, cache_vt.dtype)],
        grid=(nb // bb,),
        in_specs=[q_spec, row_spec, row_spec, col_spec, col_spec, cache_spec, cache_spec,
                  _resident((SWA_HEADS, LANES)), _resident((SWA_HEADS, LANES))],
        out_specs=[q_spec, cache_spec, cache_spec],
        compiler_params=_params(("parallel",), est), name="swa_decode")(
            q_blk, k_new.reshape(nb, 1, feat), v_new.reshape(nb, 1, feat), as_cols(k_new), as_cols(v_new),
            cache_kt, cache_vt, slopes, sinks)


def _mem_decode_body(q_ref, k_ref, v_ref, o_ref, *, bb):
    rows = q_ref.shape[1]
    flat = k_ref.shape[1] * MEM_HEADS
    col = lax.broadcasted_iota(jnp.int32, (rows, flat), 1)
    row = lax.broadcasted_iota(jnp.int32, (rows, flat), 0)
    own = (col % MEM_HEADS) == (row % MEM_HEADS)
    for b in range(bb):
        k = k_ref[b].reshape(flat, MEM_HEAD_DIM).astype(bf16)
        v = v_ref[b].reshape(flat, MEM_HEAD_DIM).astype(bf16)
        s = lax.dot_general(q_ref[b], k, NT_DIMS, preferred_element_type=f32)
        s = jnp.where(own, s, MASKED)
        p = jnp.exp(s - jnp.max(s, axis=-1, keepdims=True))
        den = jnp.sum(p, axis=-1, keepdims=True)
        o = jnp.dot(p.astype(bf16), v, preferred_element_type=f32)
        o_ref[b] = (o / den).astype(o_ref.dtype)


def _mem_decode(q, k, v, bb):
    nb, rows, hd = q.shape
    kv_block = (bb,) + k.shape[1:]
    est = 2 * (2 * _nbytes(kv_block, f32) + 2 * _nbytes((bb, rows, hd), bf16)) + 3 * _nbytes(kv_block, bf16) // bb + (4 << 20)
    body = functools.partial(_mem_decode_body, bb=bb)
    q_spec = pl.BlockSpec((bb, rows, hd), lambda i: (i, 0, 0))
    kv_spec = pl.BlockSpec(kv_block, lambda i: (i, 0, 0, 0))
    return pl.pallas_call(
        body, out_shape=SDS(q.shape, bf16), grid=(nb // bb,), in_specs=[q_spec, kv_spec, kv_spec], out_specs=q_spec,
        compiler_params=_params(("parallel",), est), name="mem_decode")(q, k, v)


def _mem_attn_body(q_ref, k_ref, v_ref, o_ref, *, bb):
    for b in range(bb):
        for h in range(MEM_HEADS):
            hs = slice(h * MEM_HEAD_DIM, (h + 1) * MEM_HEAD_DIM)
            q = q_ref[b, :, hs]
            k = k_ref[b, :, hs].astype(bf16)
            v = v_ref[b, :, hs].astype(bf16)
            s = lax.dot_general(q, k, NT_DIMS, preferred_element_type=f32)
            p = jnp.exp(s - jnp.max(s, axis=-1, keepdims=True))
            den = jnp.sum(p, axis=-1, keepdims=True)
            o = jnp.dot(p.astype(bf16), v, preferred_element_type=f32)
            o_ref[b, :, hs] = (o / den).astype(o_ref.dtype)


def _mem_attn(q, k, v, bb, tq):
    nb, t, md = q.shape
    mt = k.shape[1]
    kv_spec = pl.BlockSpec((bb, mt, md), lambda i, j: (i, 0, 0))
    est = 2 * (2 * _nbytes((bb, tq, md), bf16) + 2 * _nbytes((bb, mt, md), f32)) + 4 * _nbytes((tq, mt), f32) + (2 << 20)
    body = functools.partial(_mem_attn_body, bb=bb)
    return pl.pallas_call(
        body, out_shape=SDS((nb, t, md), bf16), grid=(nb // bb, t // tq),
        in_specs=[pl.BlockSpec((bb, tq, md), lambda i, j: (i, j, 0)), kv_spec, kv_spec],
        out_specs=pl.BlockSpec((bb, tq, md), lambda i, j: (i, j, 0)),
        compiler_params=_params(("parallel", "arbitrary"), est), name="mem_attn")(q, k, v)


def _ssd_gate_norm(y, z, gain):
    y = y * z
    gsz = y.shape[1] // SSD_GROUPS
    outs = []
    for g in range(SSD_GROUPS):
        yg = y[:, g * gsz:(g + 1) * gsz]
        ms = jnp.mean(yg * yg, axis=-1, keepdims=True)
        outs.append(yg * lax.rsqrt(ms + EPS) * gain[:, g * gsz:(g + 1) * gsz])
    return jnp.concatenate(outs, axis=-1)


def _ssd_prompt_body(xc_ref, z_ref, dt_ref, dtb_ref, alog_ref, dsk_ref, gn_ref, rexp_ref, ltri_ref,
                     y_ref, hout_ref, h_scr):
    c = pl.program_id(1)

    @pl.when(c == 0)
    def _():
        h_scr[...] = jnp.zeros_like(h_scr)

    for r0 in range(0, z_ref.shape[0], SSD_CHUNK):
        _ssd_chunk(xc_ref, z_ref, dt_ref, dtb_ref, alog_ref, dsk_ref, gn_ref, rexp_ref, ltri_ref, y_ref, h_scr, r0)

    @pl.when(c == pl.num_programs(1) - 1)
    def _():
        hout_ref[0] = h_scr[...].T


def _ssd_chunk(xc_ref, z_ref, dt_ref, dtb_ref, alog_ref, dsk_ref, gn_ref, rexp_ref, ltri_ref, y_ref, h_scr, r0):
    chunk = SSD_CHUNK
    di = z_ref.shape[1]
    gn_w = SSD_GROUPS * SSD_D_STATE
    gw = di // SSD_GROUPS
    rs = slice(r0, r0 + chunk)

    xs = xc_ref[rs, :di].astype(f32)
    bm = xc_ref[rs, di:di + gn_w]
    cm = xc_ref[rs, di + gn_w:]

    dt = _softplus(dt_ref[rs, :] + dtb_ref[...])
    da = dt * (-jnp.exp(alog_ref[...]))
    acs = jnp.dot(ltri_ref[...], da, precision=lax.Precision.HIGHEST, preferred_element_type=f32)
    acs_t = acs.T
    dt_t = dt.T
    rexp = rexp_ref[...]
    eacs_e = _split_dot(jnp.exp(acs), rexp)
    xw = (xs * _split_dot(jnp.exp(acs[chunk - 1:chunk, :] - acs) * dt, rexp)).astype(bf16)

    row = lax.broadcasted_iota(jnp.int32, (chunk, chunk), 0)
    col = lax.broadcasted_iota(jnp.int32, (chunk, chunk), 1)
    causal = row >= col
    low_half = col < SSD_HEAD_DIM
    zero = jnp.zeros((chunk, LANES), bf16)
    heads_per_group = SSD_HEADS // SSD_GROUPS
    ys = []
    for g in range(SSD_GROUPS):
        ns = slice(g * SSD_D_STATE, (g + 1) * SSD_D_STATE)
        ls = slice(g * gw, (g + 1) * gw)
        cb = lax.dot_general(cm[:, ns], bm[:, ns], NT_DIMS, preferred_element_type=f32)
        hg = h_scr[:, ls]
        y_off = jnp.dot(cm[:, ns], hg.astype(bf16), preferred_element_type=f32) * eacs_e[:, ls]
        y_diag = []
        for pr in range(heads_per_group // 2):
            h0 = g * heads_per_group + 2 * pr
            xp = xc_ref[rs, h0 * SSD_HEAD_DIM:h0 * SSD_HEAD_DIM + LANES]
            ws = []
            for hh in (h0, h0 + 1):
                diff = acs[:, hh:hh + 1] - acs_t[hh:hh + 1, :]
                ws.append((cb * jnp.exp(jnp.where(causal, diff, MASKED)) * dt_t[hh:hh + 1, :]).astype(bf16))
            x2 = jnp.concatenate([jnp.where(low_half, xp, zero), jnp.where(low_half, zero, xp)], axis=0)
            y_diag.append(jnp.dot(jnp.concatenate(ws, axis=1), x2, preferred_element_type=f32))
        st = lax.dot_general(bm[:, ns], xw[:, ls], TN_DIMS, preferred_element_type=f32)
        h_scr[:, ls] = eacs_e[chunk - 1:chunk, ls] * hg + st
        ys.append(jnp.concatenate(y_diag, axis=-1) + y_off)
    y = jnp.concatenate(ys, axis=-1) + dsk_ref[...] * xs
    y_ref[rs, :] = _ssd_gate_norm(y, z_ref[rs, :].astype(f32), gn_ref[...]).astype(y_ref.dtype)


def _ssd_prompt(xc, z, dt, ssd_w, nbatch, t, chunks_per_step):
    _, _, dtb, alog, dsk, gn, rexp, ltri = ssd_w
    ch, di = xc.shape[1], z.shape[1]
    tr = chunks_per_step * SSD_CHUNK
    nstep = t // tr
    rows = lambda b, c: (b * nstep + c, 0)
    est = (2 * (_nbytes((tr, ch), bf16) + 2 * _nbytes((tr, di), bf16) + _nbytes((di, SSD_D_STATE), f32))
           + _nbytes((SSD_D_STATE, di), f32) + 24 * _nbytes((SSD_CHUNK, ch), f32))
    return pl.pallas_call(
        _ssd_prompt_body,
        out_shape=[SDS((nbatch * t, di), bf16), SDS((nbatch, di, SSD_D_STATE), f32)],
        grid=(nbatch, nstep),
        in_specs=[pl.BlockSpec((tr, ch), rows), pl.BlockSpec((tr, di), rows),
                  pl.BlockSpec((tr, LANES), rows),
                  _resident(dtb.shape), _resident(alog.shape),
                  _resident(dsk.shape), _resident(gn.shape), _resident(rexp.shape), _resident(ltri.shape)],
        out_specs=[pl.BlockSpec((tr, di), rows), pl.BlockSpec((1, di, SSD_D_STATE), lambda b, c: (b, 0, 0))],
        scratch_shapes=[pltpu.VMEM((SSD_D_STATE, di), f32)],
        compiler_params=_params(("parallel", "arbitrary"), est), name="ssd_prompt")(xc, z, dt, dtb, alog, dsk, gn, rexp, ltri)


def _ssd_step_body(xbc_ref, z_ref, dt_ref, cs_ref, h_ref, cw_ref, cb_ref, dtb_ref, alog_ref, dsk_ref, gn_ref, rexp_ref,
                   y_ref, cso_ref, ho_ref, xdt_scr, da_scr, bm_scr, cm_scr, y_scr, *, bb):
    ch = xbc_ref.shape[1]
    di = z_ref.shape[1]
    gn_w = SSD_GROUPS * SSD_D_STATE
    gw = di // SSD_GROUPS
    halo = SSD_CONV - 1
    xr = xbc_ref[...]
    conv = cb_ref[...] + cw_ref[halo:halo + 1, :] * xr
    for k in range(halo):
        conv = conv + cw_ref[k:k + 1, :] * cs_ref[:, k * ch:(k + 1) * ch]
    for k in range(1, halo):
        cso_ref[:, (k - 1) * ch:k * ch] = cs_ref[:, k * ch:(k + 1) * ch]
    cso_ref[:, (halo - 1) * ch:] = xr
    xc = _silu(conv)
    xs = xc[:, :di]
    dt = _softplus(dt_ref[...] + dtb_ref[...])
    rexp = rexp_ref[...]
    xdt_scr[...] = xs * _split_dot(dt, rexp)
    da_scr[...] = _split_dot(jnp.exp(dt * (-jnp.exp(alog_ref[...]))), rexp)
    bm_scr[...] = xc[:, di:di + gn_w]
    cm_scr[...] = xc[:, di + gn_w:]

    rows = 2 * SUBLANES
    rowi = lax.broadcasted_iota(jnp.int32, (rows, di), 0)
    grp = lax.broadcasted_iota(jnp.int32, (rows, di), 1) // gw
    rown = lax.broadcasted_iota(jnp.int32, (rows, SSD_D_STATE), 0)
    ones_rows = jnp.where((rown == SSD_GROUPS) | (rown == SSD_GROUPS + 1), 1.0, 0.0).astype(f32)

    def per_row(b, carry):
        xrow = xdt_scr[pl.ds(b, 1), :]
        drow = da_scr[pl.ds(b, 1), :]
        d_hi = drow.astype(bf16).astype(f32)
        lhs = jnp.where(rowi == grp, xrow, jnp.where(rowi == SSD_GROUPS, d_hi, jnp.where(rowi == SSD_GROUPS + 1, drow - d_hi, 0.0)))
        brow = bm_scr[pl.ds(b, 1), :]
        crow = cm_scr[pl.ds(b, 1), :]
        rhs_b = jnp.zeros((rows, SSD_D_STATE), f32)
        c_rows = jnp.zeros((rows, SSD_D_STATE), f32)
        for g in range(SSD_GROUPS):
            ns = slice(g * SSD_D_STATE, (g + 1) * SSD_D_STATE)
            rhs_b = jnp.where(rown == g, brow[:, ns], rhs_b)
            c_rows = jnp.where(rown == g, crow[:, ns], c_rows)
        rhs = jnp.concatenate([rhs_b, ones_rows], axis=1).astype(bf16)
        sd = lax.dot_general(lhs.astype(bf16), rhs, TN_DIMS, preferred_element_type=f32)
        hn = sd[:, SSD_D_STATE:] * h_ref[b] + sd[:, :SSD_D_STATE]
        ho_ref[b] = hn
        y8 = lax.dot_general(c_rows.astype(bf16), hn.astype(bf16), NT_DIMS, preferred_element_type=f32)
        y_scr[pl.ds(b, 1), :] = jnp.sum(jnp.where(rowi == grp, y8, 0.0), axis=0, keepdims=True)
        return carry

    lax.fori_loop(0, bb, per_row, 0, unroll=True)
    y = y_scr[...] + dsk_ref[...] * xs
    y_ref[...] = _ssd_gate_norm(y, _silu(z_ref[...]), gn_ref[...]).astype(y_ref.dtype)


def _ssd_step(xbc, z, dt, conv_state, h0, ssd_w, bb):
    cw, cb, dtb, alog, dsk, gn, rexp, _ = ssd_w
    nb, ch = xbc.shape
    di = z.shape[1]
    gn_w = SSD_GROUPS * SSD_D_STATE
    halo = SSD_CONV - 1
    est = (2 * (2 * _nbytes((bb, di, SSD_D_STATE), f32) + 2 * _nbytes((bb, halo * ch), f32) + 4 * _nbytes((bb, ch), f32))
           + 8 * _nbytes((di, 2 * SSD_D_STATE), f32))
    body = functools.partial(_ssd_step_body, bb=bb)
    r2 = lambda i: (i, 0)
    return pl.pallas_call(
        body,
        out_shape=[SDS((nb, di), bf16), SDS((nb, halo * ch), f32), SDS((nb, di, SSD_D_STATE), f32)],
        grid=(nb // bb,),
        in_specs=[pl.BlockSpec((bb, ch), r2), pl.BlockSpec((bb, di), r2), pl.BlockSpec((bb, LANES), r2),
                  pl.BlockSpec((bb, halo * ch), r2), pl.BlockSpec((bb, di, SSD_D_STATE), lambda i: (i, 0, 0)),
                  _resident(cw.shape), _resident(cb.shape), _resident(dtb.shape), _resident(alog.shape),
                  _resident(dsk.shape), _resident(gn.shape), _resident(rexp.shape)],
        out_specs=[pl.BlockSpec((bb, di), r2), pl.BlockSpec((bb, halo * ch), r2),
                   pl.BlockSpec((bb, di, SSD_D_STATE), lambda i: (i, 0, 0))],
        scratch_shapes=[pltpu.VMEM((bb, di), f32), pltpu.VMEM((bb, di), f32), pltpu.VMEM((bb, gn_w), f32),
                        pltpu.VMEM((bb, gn_w), f32), pltpu.VMEM((bb, di), f32)],
        compiler_params=_params(("parallel",), est), name="ssd_step")(xbc, z, dt, conv_state, h0, cw, cb, dtb, alog, dsk, gn, rexp)


def _merge_body(a_ref, s_ref, m_ref, g0_ref, g1_ref, g2_ref, wa_ref, ws_ref, wm_ref, o_ref, *, chunk):
    a, s, m = a_ref[...], s_ref[...], m_ref[...]
    for c0 in range(0, o_ref.shape[1], chunk):
        cs = slice(c0, c0 + chunk)
        acc = g0_ref[:, cs].astype(f32) * jnp.dot(a, wa_ref[:, cs], preferred_element_type=f32)
        acc = acc + g1_ref[:, cs].astype(f32) * jnp.dot(s, ws_ref[:, cs], preferred_element_type=f32)
        acc = acc + g2_ref[:, cs].astype(f32) * jnp.dot(m, wm_ref[:, cs], preferred_element_type=f32)
        o_ref[:, cs] = acc.astype(o_ref.dtype)


def _merge(a, s, mo, gates, wa, ws, wm, tm):
    m, d = s.shape[0], wa.shape[1]
    chunk = 512
    est = (2 * (_nbytes((tm, a.shape[1]), bf16) + _nbytes((tm, s.shape[1]), bf16) + _nbytes((tm, mo.shape[1]), bf16)
                + 4 * _nbytes((tm, d), bf16))
           + _nbytes(wa.shape, bf16) + _nbytes(ws.shape, bf16) + _nbytes(wm.shape, bf16) + 6 * _nbytes((tm, chunk), f32))
    body = functools.partial(_merge_body, chunk=chunk)
    return pl.pallas_call(
        body, out_shape=SDS((m, d), bf16), grid=(m // tm,),
        in_specs=[pl.BlockSpec((tm, a.shape[1]), lambda i: (i, 0)), pl.BlockSpec((tm, s.shape[1]), lambda i: (i, 0)),
                  pl.BlockSpec((tm, mo.shape[1]), lambda i: (i, 0)),
                  pl.BlockSpec((tm, d), lambda i: (i, 0)), pl.BlockSpec((tm, d), lambda i: (i, 1)),
                  pl.BlockSpec((tm, d), lambda i: (i, 2)),
                  _resident(wa.shape), _resident(ws.shape), _resident(wm.shape)],
        out_specs=pl.BlockSpec((tm, d), lambda i: (i, 0)),
        compiler_params=_params(("parallel",), est), name="merge")(a, s, mo, gates, gates, gates, wa, ws, wm)


def _out_proj_body(m_ref, w_ref, x_ref, g_ref, x1_ref, h2_ref, *, chunk):
    m = m_ref[...]
    for c0 in range(0, x1_ref.shape[1], chunk):
        cs = slice(c0, c0 + chunk)
        x1_ref[:, cs] = x_ref[:, cs] + jnp.dot(m, w_ref[:, cs], preferred_element_type=f32)
    x1 = x1_ref[...]
    ms = jnp.mean(x1 * x1, axis=-1, keepdims=True)
    h2_ref[...] = (x1 * lax.rsqrt(ms + EPS) * g_ref[...]).astype(h2_ref.dtype)


def _out_proj(merged, w, x, gain, tm):
    m, d = x.shape
    rows = pl.BlockSpec((tm, d), lambda i: (i, 0))
    est = 2 * (_nbytes((tm, d), bf16) + 2 * _nbytes((tm, d), f32) + _nbytes((tm, d), bf16)) + _nbytes(w.shape, bf16) + (4 << 20)
    body = functools.partial(_out_proj_body, chunk=512)
    return pl.pallas_call(
        body, out_shape=[SDS((m, d), f32), SDS((m, d), bf16)], grid=(m // tm,),
        in_specs=[rows, _resident(w.shape), rows, _resident((1, d))], out_specs=[rows, rows],
        compiler_params=_params(("parallel",), est), name="out_proj")(merged, w, x, gain.reshape(1, d))


def _ffn_act_body(h_ref, wg_ref, wu_ref, a_ref, *, chunk):
    h = h_ref[...]
    for s0 in range(0, a_ref.shape[1], chunk):
        cs = slice(s0, s0 + chunk)
        gate = jnp.dot(h, wg_ref[:, cs], preferred_element_type=f32)
        up = jnp.dot(h, wu_ref[:, cs], preferred_element_type=f32)
        a_ref[:, cs] = (_silu(gate) * up).astype(a_ref.dtype)


def _ffn_act(h2, wg, wu, tm, tf):
    m, d = h2.shape
    dff = wg.shape[1]
    w_spec = pl.BlockSpec((d, tf), lambda c, i: (0, c), pipeline_mode=pl.Buffered(1))
    est = 2 * _nbytes((d, tf), bf16) + 2 * (_nbytes((tm, d), bf16) + _nbytes((tm, tf), bf16)) + (6 << 20)
    body = functools.partial(_ffn_act_body, chunk=256)
    return pl.pallas_call(
        body, out_shape=SDS((m, dff), bf16), grid=(dff // tf, m // tm),
        in_specs=[pl.BlockSpec((tm, d), lambda c, i: (i, 0)), w_spec, w_spec],
        out_specs=pl.BlockSpec((tm, tf), lambda c, i: (i, c)),
        compiler_params=_params(("arbitrary", "arbitrary"), est), name="ffn_act")(h2, wg, wu)


def _ffn_down_body(a_ref, w_ref, x_ref, o_ref, *, chunk):
    a = a_ref[...]
    for c0 in range(0, o_ref.shape[1], chunk):
        cs = slice(c0, c0 + chunk)
        o_ref[:, cs] = x_ref[:, cs] + jnp.dot(a, w_ref[:, cs], preferred_element_type=f32)


def _ffn_down(act, wd, x1, tm):
    m, d = x1.shape
    dff = act.shape[1]
    rows = pl.BlockSpec((tm, d), lambda i: (i, 0))
    est = 2 * (_nbytes((tm, dff), bf16) + 2 * _nbytes((tm, d), f32)) + _nbytes(wd.shape, bf16) + (2 << 20)
    body = functools.partial(_ffn_down_body, chunk=512)
    return pl.pallas_call(
        body, out_shape=SDS((m, d), f32), grid=(m // tm,),
        in_specs=[pl.BlockSpec((tm, dff), lambda i: (i, 0)), _resident(wd.shape), rows], out_specs=rows,
        compiler_params=_params(("parallel",), est), name="ffn_down")(act, wd, x1)


def _dup_heads(w):
    lead = w.shape[:-1]
    w = w.reshape(*lead, SWA_KV_HEADS, 1, SWA_HEAD_DIM)
    return jnp.broadcast_to(w, (*lead, SWA_KV_HEADS, 2, SWA_HEAD_DIM)).reshape(*lead, SWA_KV_HEADS * LANES)


def _pad_lanes(v, n=LANES):
    return jnp.pad(v, [(0, 0)] * (v.ndim - 1) + [(0, n - v.shape[-1])])


def _layer(xp, xs, cache_k, cache_v, cmem_k, cmem_v, state_ssm, state_conv, mem_prompt, w):
    bp, tp, d = xp.shape
    bs = xs.shape[0]
    w_buf = cache_k.shape[1]
    assert xs.shape[1] == 1 and w_buf == WINDOW and tp % SSD_CHUNK == 0 and bs % SUBLANES == 0
    q_dim = SWA_HEADS * SWA_HEAD_DIM
    kv_dim = SWA_KV_HEADS * SWA_HEAD_DIM
    di = SSD_HEADS * SSD_HEAD_DIM
    ch = di + 2 * SSD_GROUPS * SSD_D_STATE
    md = MEM_HEADS * MEM_HEAD_DIM
    o = 0
    w_in = w["w_in"].astype(bf16)
    wq, o = w_in[:, o:o + q_dim], o + q_dim
    wk, o = w_in[:, o:o + kv_dim], o + kv_dim
    wv, o = w_in[:, o:o + kv_dim], o + kv_dim
    wz, o = w_in[:, o:o + di], o + di
    wxbc, o = w_in[:, o:o + ch], o + ch
    wdt, o = w_in[:, o:o + SSD_HEADS], o + SSD_HEADS
    wqm, o = w_in[:, o:o + md], o + md
    wg = w_in[:, o:]
    wdt = _pad_lanes(wdt)

    q_scale = SWA_HEAD_DIM ** -0.5
    gq = jnp.tile(w["q_norm_swa"], SWA_HEADS) * q_scale
    gk = jnp.tile(w["k_norm_swa"], SWA_KV_HEADS)
    head_of = np.arange(2 * LANES) // SWA_HEAD_DIM
    blockdiag = jnp.asarray(head_of[:, None] == head_of[None, :], bf16)
    w_qkv_p = jnp.concatenate([wq, _dup_heads(wk), _dup_heads(wv), wdt], axis=1)
    w_qkv_s = jnp.concatenate([wq, wk, wv, wdt], axis=1)
    gqm = (jnp.tile(w["q_norm_mem"], MEM_HEADS) * MEM_HEAD_DIM ** -0.5).reshape(1, md)
    gkm = jnp.tile(w["k_norm_mem"], MEM_HEADS).reshape(1, md)
    wz_b, wxbc_b, wqm_b, wg_b = wz, wxbc, wqm, wg
    w_mem_k, w_mem_v = w["w_mem_kv"][:, :md].astype(bf16), w["w_mem_kv"][:, md:].astype(bf16)
    wa, ws, wm = w["w_up_swa"].astype(bf16), w["w_up_ssd"].astype(bf16), w["w_up_mem"].astype(bf16)
    w_out = w["w_out"].astype(bf16)
    w_gate, w_up, w_down = w["w_gate"].astype(bf16), w["w_up"].astype(bf16), w["w_down"].astype(bf16)

    head_of_lane = np.arange(di) // SSD_HEAD_DIM
    rexp = jnp.asarray(np.arange(LANES)[:, None] == head_of_lane[None, :], bf16)
    ltri = jnp.asarray(np.arange(SSD_CHUNK)[:, None] >= np.arange(SSD_CHUNK)[None, :], f32)
    ssd_w = (w["conv_w"], w["conv_b"].reshape(1, ch), _pad_lanes(w["dt_bias"].reshape(1, -1)),
             _pad_lanes(w["a_log"].reshape(1, -1)), jnp.repeat(w["d_skip"], SSD_HEAD_DIM).reshape(1, di),
             w["ssd_norm"].reshape(1, di), rexp, ltri)
    slopes = jnp.asarray(np.broadcast_to(np.exp2(-8.0 * np.arange(1, SWA_HEADS + 1) / SWA_HEADS)[:, None], (SWA_HEADS, LANES)), f32)
    sinks_b = jnp.broadcast_to(w["swa_sinks"].astype(f32)[:, None], (SWA_HEADS, LANES))

    dff = w_gate.shape[1]
    tf = dff // 2 if dff % (4 * LANES) == 0 else dff

    def finish(x2, a_out, s_out, m_out, gates, tm, tm_act):
        merged = _merge(a_out, s_out, m_out, gates, wa, ws, wm, tm)
        x1, h2 = _out_proj(merged, w_out, x2, w["norm_ffn"], tm)
        act = _ffn_act(h2, w_gate, w_up, tm_act, tf)
        return _ffn_down(act, w_down, x1, tm)

    mp = bp * tp
    x2 = xp.reshape(mp, d)
    tm = min(1024, tp)
    tpb = tp // tm
    tm_qkv = min(512, tp)
    g_mix = w["norm_mix"].reshape(1, d)
    h, q, kd, v4, dt, klast, vlast = _qkv_proj(
        x2, g_mix, w_qkv_p, (gq * LOG2_E).reshape(1, -1), _dup_heads(gk).reshape(1, -1), blockdiag,
        q_dim, 2 * kv_dim, 2 * kv_dim, tm_qkv, tp // tm_qkv, bf16, bf16, True)
    z = _proj(h, wz_b, _epi_silu, bf16, tm, di, name="z_proj")
    xc, xtail = _xbc_proj(h, wxbc_b, ssd_w[0], ssd_w[1], tm, tpb)
    qm = _proj(h, wqm_b, _epi_headnorm256, bf16, tm, md, aux=(gqm,), aux_specs=(_resident((1, md)),), name="qm_proj")
    gates = _proj(h, wg_b, _epi_sigmoid, bf16, tm, d, name="gate_proj")
    a_out = _swa_prompt(q, kd, v4, w["swa_sinks"].astype(f32) * LOG2_E, bp, tp)
    s_out, p_h = _ssd_prompt(xc, z, dt, ssd_w, bp, tp, 4 if tp % (4 * SSD_CHUNK) == 0 else 1)
    mt = mem_prompt.shape[1]
    hm = _rmsnorm(mem_prompt.reshape(bp * mt, d), w["norm_mem"], mt)
    mk = _proj(hm, w_mem_k, _epi_headnorm256, f32, mt, md, aux=(gkm,), aux_specs=(_resident((1, md)),), name="mem_k_proj")
    mv = _proj(hm, w_mem_v, _epi_plain, f32, mt, md, name="mem_v_proj")
    m_out = _mem_attn(qm.reshape(bp, tp, md), mk.reshape(bp, mt, md), mv.reshape(bp, mt, md), 1, tm)
    yp = finish(x2, a_out, s_out, m_out.reshape(mp, md), gates, min(512, tp), tm).reshape(bp, tp, d)

    undup = lambda t: t.reshape(bp, WINDOW, SWA_KV_HEADS, 2, SWA_HEAD_DIM)[:, :, :, 0, :]
    p_k, p_v = undup(klast), undup(vlast)
    p_mk = mk.reshape(bp, mt, MEM_HEADS, MEM_HEAD_DIM)
    p_mv = mv.reshape(bp, mt, MEM_HEADS, MEM_HEAD_DIM)
    p_h = p_h.reshape(bp, SSD_HEADS, SSD_HEAD_DIM, SSD_D_STATE)
    p_c = xtail.reshape(bp, SUBLANES, ch)[:, SUBLANES - (SSD_CONV - 1):, :]

    xs2 = xs.reshape(bs, d)
    hs, qs, ks, vs, dts = _qkv_proj(xs2, g_mix, w_qkv_s, gq.reshape(1, -1), gk.reshape(1, -1), blockdiag,
                                    q_dim, kv_dim, kv_dim, bs, 1, f32, f32, False)
    zs = _proj(hs, wz_b, _epi_plain, f32, bs, di, name="z_proj_s")
    xbcs = _proj(hs, wxbc_b, _epi_plain, f32, bs, ch, name="xbc_proj_s")
    qms = _proj(hs, wqm_b, _epi_headnorm256, bf16, bs, md, aux=(gqm,), aux_specs=(_resident((1, md)),), name="qm_proj_s")
    gates_s = _proj(hs, wg_b, _epi_sigmoid, bf16, bs, d, name="gate_proj_s")
    own_kv = jnp.asarray((np.arange(SWA_HEADS) // SWA_GROUP)[:, None] == np.arange(SWA_KV_HEADS)[None, :], f32)
    q_blk = (qs.reshape(bs, SWA_HEADS, 1, SWA_HEAD_DIM) * own_kv[None, :, :, None]).reshape(bs, SWA_HEADS, kv_dim)
    a_blk, s_kt, s_vt = _swa_decode(q_blk, ks, vs, jnp.transpose(cache_k, (0, 2, 3, 1)), jnp.transpose(cache_v, (0, 2, 3, 1)),
                                    slopes, sinks_b, SUBLANES)
    a_s = jnp.sum(a_blk.reshape(bs, SWA_HEADS, SWA_KV_HEADS, SWA_HEAD_DIM), axis=2).astype(bf16)
    s_k, s_v = jnp.transpose(s_kt, (0, 3, 1, 2)), jnp.transpose(s_vt, (0, 3, 1, 2))
    halo = SSD_CONV - 1
    s_s, s_c, s_h = _ssd_step(xbcs, zs, dts, state_conv.reshape(bs, halo * ch), state_ssm.reshape(bs, di, SSD_D_STATE),
                              ssd_w, SUBLANES)
    qm_rows = jnp.tile(qms.reshape(bs, MEM_HEADS, MEM_HEAD_DIM), (1, 2 * SUBLANES // MEM_HEADS, 1))
    m_s = _mem_decode(qm_rows, cmem_k, cmem_v, 4)[:, :MEM_HEADS, :].reshape(bs, md)
    ys = finish(xs2, a_s.reshape(bs, q_dim), s_s, m_s, gates_s, bs, bs).reshape(bs, 1, d)

    s_h =s_h.reshape(bs, SSD_HEADS, SSD_HEAD_DIM, SSD_D_STATE)
    s_c = s_c.reshape(bs, halo, ch)
    return yp, ys, (p_k, p_v, p_mk, p_mv, p_h, p_c), (s_k, s_v, s_h, s_c)


_WEIGHT_NAMES = ("norm_mix", "w_in", "q_norm_swa", "k_norm_swa", "swa_sinks", "conv_w", "conv_b", "dt_bias", "a_log",
                 "d_skip", "ssd_norm", "norm_mem", "w_mem_kv", "q_norm_mem", "k_norm_mem", "w_up_swa", "w_up_ssd",
                 "w_up_mem", "w_out", "norm_ffn", "w_gate", "w_up", "w_down")


def kernel(x_prompt, x_sample, cache_swa_k, cache_swa_v, cache_mem_k, cache_mem_v, state_ssm, state_conv, mem_prompt, norm_mix, w_in, q_norm_swa, k_norm_swa, swa_sinks, conv_w, conv_b, dt_bias, a_log, d_skip, ssd_norm, norm_mem, w_mem_kv, q_norm_mem, k_norm_mem, w_up_swa, w_up_ssd, w_up_mem, w_out, norm_ffn, w_gate, w_up, w_down):
    weights = (norm_mix, w_in, q_norm_swa, k_norm_swa, swa_sinks, conv_w, conv_b, dt_bias, a_log, d_skip, ssd_norm,
               norm_mem, w_mem_kv, q_norm_mem, k_norm_mem, w_up_swa, w_up_ssd, w_up_mem, w_out, norm_ffn, w_gate, w_up, w_down)
    depth = w_in.shape[0]
    layer = (lambda a, l: a.reshape(a.shape[1:])) if depth == 1 else (lambda a, l: a[l])
    yp, ys = x_prompt, x_sample
    p_outs, s_outs = [], []
    for l in range(depth):
        w = {n: layer(a, l) for n, a in zip(_WEIGHT_NAMES, weights)}
        yp, ys, po, so = _layer(yp, ys, layer(cache_swa_k, l), layer(cache_swa_v, l), layer(cache_mem_k, l),
                                layer(cache_mem_v, l), layer(state_ssm, l), layer(state_conv, l), mem_prompt, w)
        p_outs.append(po)
        s_outs.append(so)
    stack = lambda outs, i: jnp.stack([o[i] for o in outs])
    return (yp, ys, *(stack(p_outs, i) for i in range(6)), *(stack(s_outs, i) for i in range(4)))
```

```python
import functools

import jax
import jax.numpy as jnp
import numpy as np
from jax import lax
from jax.experimental import pallas as pl
from jax.experimental.pallas import tpu as pltpu

f32 = jnp.float32
bf16 = jnp.bfloat16
SDS = jax.ShapeDtypeStruct

LANES = 128
SUBLANES = 8
VMEM_BYTES_V7X = 64 * 1024 * 1024
VMEM_HEADROOM = 8 * 1024 * 1024

EPS = 1e-6
SWA_HEADS = 16
SWA_KV_HEADS = 4
SWA_GROUP = SWA_HEADS // SWA_KV_HEADS
SWA_HEAD_DIM = 64
WINDOW = 128
SWA_BLOCK = 128
SSD_HEAD_DIM = 64
SSD_HEADS = 32
SSD_GROUPS = 4
SSD_D_STATE = 128
SSD_CONV = 4
SSD_CHUNK = 128
MEM_HEADS = 4
MEM_HEAD_DIM = 256
N_BRANCH = 3
MASKED = -1e30
LOG2_E = 1.4426950408889634

NT_DIMS = (((1,), (1,)), ((), ()))
TN_DIMS = (((0,), (0,)), ((), ()))


def _params(semantics, block_bytes):
    limit = min(int(block_bytes) + VMEM_HEADROOM, VMEM_BYTES_V7X - VMEM_HEADROOM)
    return pltpu.CompilerParams(dimension_semantics=semantics, vmem_limit_bytes=limit)


def _nbytes(shape, dtype):
    n = 1
    for s in shape:
        n *= s
    return n * jnp.dtype(dtype).itemsize


def _resident(shape):
    nd = len(shape)
    return pl.BlockSpec(shape, lambda *_: (0,) * nd, pipeline_mode=pl.Buffered(1))


def _sigmoid(x):
    return 0.5 + 0.5 * jnp.tanh(0.5 * x)


def _silu(x):
    hx = 0.5 * x
    return hx + hx * jnp.tanh(hx)


def _softplus(x):
    return jnp.maximum(x, 0.0) + jnp.log1p(jnp.exp(-jnp.abs(x)))


def _split_dot(v, m):
    hi = v.astype(bf16)
    lo = (v - hi.astype(f32)).astype(bf16)
    return jnp.dot(hi, m, preferred_element_type=f32) + jnp.dot(lo, m, preferred_element_type=f32)


def _headnorm64(acc, gain, blockdiag):
    outs = []
    wide = blockdiag.shape[0]
    for c in range(acc.shape[1] // wide):
        a = acc[:, c * wide:(c + 1) * wide]
        ss = jnp.dot((a * a).astype(bf16), blockdiag, preferred_element_type=f32)
        outs.append(a * lax.rsqrt(ss * (1.0 / SWA_HEAD_DIM) + EPS) * gain[:, c * wide:(c + 1) * wide])
    return outs[0] if len(outs) == 1 else jnp.concatenate(outs, axis=-1)


def _norm_body(x_ref, g_ref, o_ref):
    x = x_ref[...]
    ms = jnp.mean(x * x, axis=-1, keepdims=True)
    o_ref[...] = (x * lax.rsqrt(ms + EPS) * g_ref[...]).astype(o_ref.dtype)


def _rmsnorm(x, gain, tm):
    m, d = x.shape
    est = 2 * (_nbytes((tm, d), f32) + _nbytes((tm, d), bf16)) + _nbytes((tm, d), f32)
    return pl.pallas_call(
        _norm_body, out_shape=SDS((m, d), bf16), grid=(m // tm,),
        in_specs=[pl.BlockSpec((tm, d), lambda i: (i, 0)), _resident((1, d))],
        out_specs=pl.BlockSpec((tm, d), lambda i: (i, 0)),
        compiler_params=_params(("parallel",), est), name="rmsnorm")(x, gain.reshape(1, d))


def _qkv_body(x_ref, gx_ref, w_ref, gq_ref, gk_ref, bd_ref, h_ref, q_ref, k_ref, *rest, nq, nk, nv, chunk, prompt):
    v_ref, dt_ref, *last_refs = rest
    x = x_ref[...]
    h = (x * lax.rsqrt(jnp.mean(x * x, axis=-1, keepdims=True) + EPS) * gx_ref[...]).astype(bf16)
    h_ref[...] = h
    bd = bd_ref[...]
    tm = h.shape[0]
    for c0 in range(0, nq, chunk):
        acc = jnp.dot(h, w_ref[:, c0:c0 + chunk], preferred_element_type=f32)
        q_ref[:, c0:c0 + chunk] = _headnorm64(acc, gq_ref[:, c0:c0 + chunk], bd).astype(q_ref.dtype)
    for c0 in range(0, nk, chunk):
        w = min(chunk, nk - c0)
        acc = jnp.dot(h, w_ref[:, nq + c0:nq + c0 + w], preferred_element_type=f32)
        kn = _headnorm64(acc, gk_ref[:, c0:c0 + w], bd)
        k_ref[:, c0:c0 + w] = kn.astype(k_ref.dtype)
        if last_refs:
            last_refs[0][:, c0:c0 + w] = kn[tm - WINDOW:, :]
    for c0 in range(0, nv, chunk):
        w = min(chunk, nv - c0)
        acc = jnp.dot(h, w_ref[:, nq + nk + c0:nq + nk + c0 + w], preferred_element_type=f32)
        if prompt:
            low_half = lax.broadcasted_iota(jnp.int32, (tm, LANES), 1) < SWA_HEAD_DIM
            for t0 in range(0, w, LANES):
                a = acc[:, t0:t0 + LANES]
                v_ref[:, 2 * (c0 + t0):2 * (c0 + t0) + LANES] = jnp.where(low_half, a, 1.0).astype(v_ref.dtype)
                v_ref[:, 2 * (c0 + t0) + LANES:2 * (c0 + t0 + LANES)] = jnp.where(low_half, 1.0, a).astype(v_ref.dtype)
            last_refs[1][:, c0:c0 + w] = acc[tm - WINDOW:, :]
        else:
            v_ref[:, c0:c0 + w] = acc.astype(v_ref.dtype)
    dt_ref[...] = jnp.dot(h, w_ref[:, nq + nk + nv:], preferred_element_type=f32)


def _qkv_proj(x, gx, w, gq, gk, blockdiag, nq, nk, nv, tm, tiles_per_batch, q_dtype, kv_dtype, prompt):
    m, d = x.shape
    n = w.shape[1]
    chunk = 512
    rows = lambda width: pl.BlockSpec((tm, width), lambda i: (i, 0))
    v_width = 2 * nv if prompt else nv
    out_shape = [SDS((m, d), bf16), SDS((m, nq), q_dtype), SDS((m, nk), kv_dtype), SDS((m, v_width), kv_dtype),
                 SDS((m, LANES), f32)]
    out_specs = [rows(d), rows(nq), rows(nk), rows(v_width), rows(LANES)]
    if prompt:
        nb = m // (tm * tiles_per_batch)
        out_shape += [SDS((nb * WINDOW, nk), f32), SDS((nb * WINDOW, nv), f32)]
        out_specs += [pl.BlockSpec((WINDOW, nk), lambda i: (i // tiles_per_batch, 0)),
                      pl.BlockSpec((WINDOW, nv), lambda i: (i // tiles_per_batch, 0))]
    est = (2 * _nbytes((tm, d), f32) + 3 * _nbytes((tm, d), bf16) + _nbytes((d, n), bf16)
           + 2 * _nbytes((tm, nq + nk + 2 * nv), f32) + 4 * _nbytes((tm, chunk), f32))
    body = functools.partial(_qkv_body, nq=nq, nk=nk, nv=nv, chunk=chunk, prompt=prompt)
    return pl.pallas_call(
        body, out_shape=out_shape, grid=(m // tm,),
        in_specs=[pl.BlockSpec((tm, d), lambda i: (i, 0)), _resident((1, d)), _resident((d, n)), _resident((1, nq)),
                  _resident((1, nk)), _resident(blockdiag.shape)],
        out_specs=out_specs, compiler_params=_params(("arbitrary",), est), name="qkv_proj")(x, gx, w, gq, gk, blockdiag)


def _proj_body(h_ref, w_ref, *refs, epilogue, chunk, n_aux):
    aux, outs = refs[:n_aux], refs[n_aux:]
    h = h_ref[...]
    tn = w_ref.shape[1]
    for c0 in range(0, tn, chunk):
        acc = jnp.dot(h, w_ref[:, c0:c0 + chunk], preferred_element_type=f32)
        epilogue(acc, c0, chunk, aux, outs)


def _epi_plain(acc, c0, w, aux, outs):
    outs[0][:, c0:c0 + w] = acc.astype(outs[0].dtype)


def _epi_silu(acc, c0, w, aux, outs):
    outs[0][:, c0:c0 + w] = _silu(acc).astype(outs[0].dtype)


def _epi_sigmoid(acc, c0, w, aux, outs):
    outs[0][:, c0:c0 + w] = _sigmoid(acc).astype(outs[0].dtype)


def _epi_headnorm256(acc, c0, w, aux, outs):
    gain = aux[0]
    for h0 in range(0, w, MEM_HEAD_DIM):
        a = acc[:, h0:h0 + MEM_HEAD_DIM]
        ms = jnp.mean(a * a, axis=-1, keepdims=True)
        y = a * lax.rsqrt(ms + EPS) * gain[:, c0 + h0:c0 + h0 + MEM_HEAD_DIM]
        outs[0][:, c0 + h0:c0 + h0 + MEM_HEAD_DIM] = y.astype(outs[0].dtype)


def _proj(h, w, epilogue, out_dtype, tm, tn, aux=(), aux_specs=(), chunk=512, name="proj"):
    m, d = h.shape
    n = w.shape[1]
    nj = n // tn
    chunk = min(chunk, tn)
    w_spec = _resident((d, n)) if nj == 1 else pl.BlockSpec((d, tn), lambda i, j: (0, j))
    est = (2 * _nbytes((tm, d), bf16) + (1 if nj == 1 else 2) * _nbytes((d, tn), bf16)
           + 2 * _nbytes((tm, tn), out_dtype) + 4 * _nbytes((tm, chunk), f32)
           + sum(2 * _nbytes(s.block_shape, f32) for s in aux_specs))
    body = functools.partial(_proj_body, epilogue=epilogue, chunk=chunk, n_aux=len(aux))
    return pl.pallas_call(
        body, out_shape=SDS((m, n), out_dtype), grid=(m // tm, nj),
        in_specs=[pl.BlockSpec((tm, d), lambda i, j: (i, 0)), w_spec, *aux_specs],
        out_specs=pl.BlockSpec((tm, tn), lambda i, j: (i, j)),
        compiler_params=_params(("parallel", "arbitrary"), est), name=name)(h, w, *aux)


def _xbc_body(h_ref, w_ref, cw_ref, cb_ref, o_ref, tail_ref, halo_scr, xbuf, *, chunk, tiles_per_batch):
    tm = h_ref.shape[0]
    halo = SSD_CONV - 1

    @pl.when(lax.rem(pl.program_id(0), tiles_per_batch) == 0)
    def _():
        halo_scr[...] = jnp.zeros_like(halo_scr)

    h = h_ref[...]
    for c0 in range(0, w_ref.shape[1], chunk):
        cs = slice(c0, c0 + chunk)
        acc = jnp.dot(h, w_ref[:, cs], preferred_element_type=f32)
        last = acc[tm - SUBLANES:, :]
        tail_ref[:, cs] = last
        xbuf[0:SUBLANES, :] = halo_scr[:, cs]
        xbuf[SUBLANES:, :] = acc
        halo_scr[:, cs] = last
        conv = cb_ref[:, cs] + cw_ref[halo:halo + 1, cs] * acc
        for k in range(halo):
            conv = conv + cw_ref[k:k + 1, cs] * xbuf[SUBLANES - halo + k:SUBLANES - halo + k + tm, :]
        o_ref[:, cs] = _silu(conv).astype(o_ref.dtype)


def _xbc_proj(h, w, conv_w, conv_b, tm, tiles_per_batch):
    m, d = h.shape
    n = w.shape[1]
    chunk = 512
    nb = m // (tm * tiles_per_batch)
    est = (2 * _nbytes((tm, d), bf16) + _nbytes((d, n), bf16) + 2 * _nbytes((tm, n), bf16)
           + 8 * _nbytes((tm, chunk), f32))
    body = functools.partial(_xbc_body, chunk=chunk, tiles_per_batch=tiles_per_batch)
    return pl.pallas_call(
        body, out_shape=[SDS((m, n), bf16), SDS((nb * SUBLANES, n), f32)], grid=(m // tm,),
        in_specs=[pl.BlockSpec((tm, d), lambda i: (i, 0)), _resident((d, n)), _resident(conv_w.shape),
                  _resident(conv_b.shape)],
        out_specs=[pl.BlockSpec((tm, n), lambda i: (i, 0)),
                   pl.BlockSpec((SUBLANES, n), lambda i: (i // tiles_per_batch, 0))],
        scratch_shapes=[pltpu.VMEM((SUBLANES, n), f32), pltpu.VMEM((SUBLANES + tm, chunk), f32)],
        compiler_params=_params(("arbitrary",), est), name="xbc_proj")(h, w, conv_w, conv_b)


def _swa_prompt_body(sink_ref, q_ref, k_ref, v_ref, o_ref, bias_scr, *, nblk):
    blk = SWA_BLOCK

    @pl.when(pl.program_id(0) == 0)
    def _():
        row = lax.broadcasted_iota(jnp.int32, (blk, 2 * blk), 0)
        col = lax.broadcasted_iota(jnp.int32, (blk, 2 * blk), 1)
        dist = row + blk - col
        allowed = (dist >= 0) & (dist <= WINDOW)
        distf = dist.astype(f32)
        for hh in range(SWA_HEADS):
            slope = 2.0 ** (-8.0 * (hh + 1) / SWA_HEADS) * LOG2_E
            bias = jnp.where(allowed, -slope * distf, MASKED)
            bias_scr[1, hh] = bias
            bias_scr[0, hh] = jnp.where(col >= blk, bias, MASKED)

    lane = lax.broadcasted_iota(jnp.int32, (blk, LANES), 1)
    low_half = lane < SWA_HEAD_DIM
    zero = jnp.zeros((blk, LANES), bf16)

    def block(j, carry):
        r0 = pl.multiple_of(j * blk, blk)
        rp = pl.multiple_of(jnp.maximum(j - 1, 0) * blk, blk)
        first = jnp.minimum(j, 1)
        for g in range(SWA_KV_HEADS):
            ks = slice(g * LANES, (g + 1) * LANES)
            vs = slice(2 * g * LANES, 2 * (g + 1) * LANES)
            kcat = jnp.concatenate([k_ref[pl.ds(rp, blk), ks], k_ref[pl.ds(r0, blk), ks]], axis=0)
            vcat = jnp.concatenate([v_ref[pl.ds(rp, blk), vs], v_ref[pl.ds(r0, blk), vs]], axis=0)
            for pr in range(SWA_GROUP // 2):
                c0 = (g * SWA_GROUP + 2 * pr) * SWA_HEAD_DIM
                q2 = q_ref[pl.ds(r0, blk), c0:c0 + LANES]
                res = []
                for half in range(2):
                    r = 2 * pr + half
                    qm = jnp.where(low_half, q2, zero) if half == 0 else jnp.where(low_half, zero, q2)
                    s = lax.dot_general(qm, kcat, NT_DIMS, preferred_element_type=f32)
                    s = s + bias_scr[first, g * SWA_GROUP + r]
                    sink = sink_ref[g * SWA_GROUP + r]
                    mx = jnp.maximum(jnp.max(s, axis=-1, keepdims=True), sink)
                    p = jnp.exp2(s - mx)
                    o = jnp.dot(p.astype(bf16), vcat, preferred_element_type=f32)
                    es = jnp.exp2(sink - mx)
                    lo, hi = o[:, :LANES], o[:, LANES:]
                    res.append(lo / (hi + es) if half == 0 else hi / (lo + es))
                o_ref[pl.ds(r0, blk), c0:c0 + LANES] = jnp.where(low_half, res[0], res[1]).astype(o_ref.dtype)
        return carry

    lax.fori_loop(0, nblk, block, 0, unroll=4 if nblk % 4 == 0 else 1)


def _swa_prompt(q, kd, v4, sinks_log2, nbatch, t):
    nq, nk, nv = q.shape[1], kd.shape[1], v4.shape[1]
    bias_shape = (2, SWA_HEADS, SWA_BLOCK, 2 * SWA_BLOCK)
    est = (2 * (2 * _nbytes((t, nq), bf16) + _nbytes((t, nk), bf16) + _nbytes((t, nv), bf16)) + _nbytes(bias_shape, f32)
           + 32 * _nbytes((SWA_BLOCK, 2 * SWA_BLOCK), f32))
    body = functools.partial(_swa_prompt_body, nblk=t // SWA_BLOCK)
    rows = lambda width: pl.BlockSpec((t, width), lambda b: (b, 0))
    return pl.pallas_call(
        body, out_shape=SDS(q.shape, bf16), grid=(nbatch,),
        in_specs=[pl.BlockSpec(memory_space=pltpu.SMEM), rows(nq), rows(nk), rows(nv)],
        out_specs=rows(nq), scratch_shapes=[pltpu.VMEM(bias_shape, f32)],
        compiler_params=_params(("arbitrary",), est), name="swa_prompt")(sinks_log2, q, kd, v4)


def _swa_decode_body(q_ref, kn_ref, vn_ref, knt_ref, vnt_ref, ck_ref, cv_ref, slope_ref, sink_ref, o_ref, ok_ref, ov_ref,
                     *, bb, w_buf):
    feat = SWA_KV_HEADS * SWA_HEAD_DIM
    lane_kv = lax.broadcasted_iota(jnp.int32, (SWA_HEADS, feat), 1) // SWA_HEAD_DIM
    row_kv = lax.broadcasted_iota(jnp.int32, (SWA_HEADS, feat), 0) // SWA_GROUP
    own = lane_kv == row_kv
    tok = lax.broadcasted_iota(jnp.int32, (SWA_HEADS, w_buf), 1)
    bias = slope_ref[:, :w_buf] * (tok - w_buf).astype(f32)
    sink = sink_ref[:, 0:1]
    last_tok = lax.broadcasted_iota(jnp.int32, (feat, w_buf), 1) == w_buf - 1

    for b in range(bb):
        q = q_ref[b].astype(bf16)
        kn = kn_ref[b].astype(bf16).astype(f32)
        vn = vn_ref[b].astype(bf16).astype(f32)
        kt = ck_ref[b].reshape(feat, w_buf)
        vt = cv_ref[b].reshape(feat, w_buf)
        s = jnp.dot(q, kt.astype(bf16), preferred_element_type=f32) + bias
        sn = jnp.sum(q.astype(f32) * kn, axis=-1, keepdims=True)
        mx = jnp.maximum(jnp.maximum(jnp.max(s, axis=-1, keepdims=True), sn), sink)
        p = jnp.exp(s - mx)
        pn = jnp.exp(sn - mx)
        den = jnp.sum(p, axis=-1, keepdims=True) + pn + jnp.exp(sink - mx)
        o = lax.dot_general(p.astype(bf16), vt.astype(bf16), NT_DIMS, preferred_element_type=f32)
        o = o + pn.astype(bf16).astype(f32) * vn
        o_ref[b] = jnp.where(own, o / den, 0.0).astype(o_ref.dtype)
        ok_ref[b] = jnp.where(last_tok, knt_ref[0, :, b:b + 1], pltpu.roll(kt, w_buf - 1, 1)).reshape(ok_ref.shape[1:])
        ov_ref[b] = jnp.where(last_tok, vnt_ref[0, :, b:b + 1], pltpu.roll(vt, w_buf - 1, 1)).reshape(ov_ref.shape[1:])


def _swa_decode(q_blk, k_new, v_new, cache_kt, cache_vt, slopes, sinks, bb):
    nb, kvh, hd, w_buf = cache_kt.shape
    feat = kvh * hd
    est = 2 * (4 * _nbytes((bb, feat, w_buf), f32) + 2 * _nbytes((bb, SWA_HEADS, feat), f32)) + (4 << 20)
    body = functools.partial(_swa_decode_body, bb=bb, w_buf=w_buf)
    cache_spec = pl.BlockSpec((bb, kvh, hd, w_buf), lambda i: (i, 0, 0, 0))
    row_spec = pl.BlockSpec((bb, 1, feat), lambda i: (i, 0, 0))
    col_spec = pl.BlockSpec((1, feat, bb), lambda i: (i, 0, 0))
    q_spec = pl.BlockSpec((bb, SWA_HEADS, feat), lambda i: (i, 0, 0))
    as_cols = lambda t: jnp.transpose(t.reshape(nb // bb, bb, feat), (0, 2, 1))
    return pl.pallas_call(
        body, out_shape=[SDS((nb, SWA_HEADS, feat), f32), SDS(cache_kt.shape, cache_kt.dtype), SDS(cache_vt.shape, cache_vt.dtype)],
        grid=(nb // bb,),
        in_specs=[q_spec, row_spec, row_spec, col_spec, col_spec, cache_spec, cache_spec,
                  _resident((SWA_HEADS, LANES)), _resident((SWA_HEADS, LANES))],
        out_specs=[q_spec, cache_spec, cache_spec],
        compiler_params=_params(("parallel",), est), name="swa_decode")(
            q_blk, k_new.reshape(nb, 1, feat), v_new.reshape(nb, 1, feat), as_cols(k_new), as_cols(v_new),
            cache_kt, cache_vt, slopes, sinks)


def _mem_decode_body(q_ref, k_ref, v_ref, o_ref, *, bb):
    rows = q_ref.shape[1]
    flat = k_ref.shape[1] * MEM_HEADS
    col = lax.broadcasted_iota(jnp.int32, (rows, flat), 1)
    row = lax.broadcasted_iota(jnp.int32, (rows, flat), 0)
    own = (col % MEM_HEADS) == (row % MEM_HEADS)
    for b in range(bb):
        k = k_ref[b].reshape(flat, MEM_HEAD_DIM).astype(bf16)
        v = v_ref[b].reshape(flat, MEM_HEAD_DIM).astype(bf16)
        s = lax.dot_general(q_ref[b], k, NT_DIMS, preferred_element_type=f32)
        s = jnp.where(own, s, MASKED)
        p = jnp.exp(s - jnp.max(s, axis=-1, keepdims=True))
        den = jnp.sum(p, axis=-1, keepdims=True)
        o = jnp.dot(p.astype(bf16), v, preferred_element_type=f32)
        o_ref[b] = (o / den).astype(o_ref.dtype)


def _mem_decode(q, k, v, bb):
    nb, rows, hd = q.shape
    kv_block = (bb,) + k.shape[1:]
    est = 2 * (2 * _nbytes(kv_block, f32) + 2 * _nbytes((bb, rows, hd), bf16)) + 3 * _nbytes(kv_block, bf16) // bb + (4 << 20)
    body = functools.partial(_mem_decode_body, bb=bb)
    q_spec = pl.BlockSpec((bb, rows, hd), lambda i: (i, 0, 0))
    kv_spec = pl.BlockSpec(kv_block, lambda i: (i, 0, 0, 0))
    return pl.pallas_call(
        body, out_shape=SDS(q.shape, bf16), grid=(nb // bb,), in_specs=[q_spec, kv_spec, kv_spec], out_specs=q_spec,
        compiler_params=_params(("parallel",), est), name="mem_decode")(q, k, v)


def _mem_attn_body(q_ref, k_ref, v_ref, o_ref, *, bb):
    for b in range(bb):
        for h in range(MEM_HEADS):
            hs = slice(h * MEM_HEAD_DIM, (h + 1) * MEM_HEAD_DIM)
            q = q_ref[b, :, hs]
            k = k_ref[b, :, hs].astype(bf16)
            v = v_ref[b, :, hs].astype(bf16)
            s = lax.dot_general(q, k, NT_DIMS, preferred_element_type=f32)
            p = jnp.exp(s - jnp.max(s, axis=-1, keepdims=True))
            den = jnp.sum(p, axis=-1, keepdims=True)
            o = jnp.dot(p.astype(bf16), v, preferred_element_type=f32)
            o_ref[b, :, hs] = (o / den).astype(o_ref.dtype)


def _mem_attn(q, k, v, bb, tq):
    nb, t, md = q.shape
    mt = k.shape[1]
    kv_spec = pl.BlockSpec((bb, mt, md), lambda i, j: (i, 0, 0))
    est = 2 * (2 * _nbytes((bb, tq, md), bf16) + 2 * _nbytes((bb, mt, md), f32)) + 4 * _nbytes((tq, mt), f32) + (2 << 20)
    body = functools.partial(_mem_attn_body, bb=bb)
    return pl.pallas_call(
        body, out_shape=SDS((nb, t, md), bf16), grid=(nb // bb, t // tq),
        in_specs=[pl.BlockSpec((bb, tq, md), lambda i, j: (i, j, 0)), kv_spec, kv_spec],
        out_specs=pl.BlockSpec((bb, tq, md), lambda i, j: (i, j, 0)),
        compiler_params=_params(("parallel", "arbitrary"), est), name="mem_attn")(q, k, v)


def _ssd_gate_norm(y, z, gain):
    y = y * z
    gsz = y.shape[1] // SSD_GROUPS
    outs = []
    for g in range(SSD_GROUPS):
        yg = y[:, g * gsz:(g + 1) * gsz]
        ms = jnp.mean(yg * yg, axis=-1, keepdims=True)
        outs.append(yg * lax.rsqrt(ms + EPS) * gain[:, g * gsz:(g + 1) * gsz])
    return jnp.concatenate(outs, axis=-1)


def _ssd_prompt_body(xc_ref, z_ref, dt_ref, dtb_ref, alog_ref, dsk_ref, gn_ref, rexp_ref, ltri_ref,
                     y_ref, hout_ref, h_scr):
    c = pl.program_id(1)

    @pl.when(c == 0)
    def _():
        h_scr[...] = jnp.zeros_like(h_scr)

    for r0 in range(0, z_ref.shape[0], SSD_CHUNK):
        _ssd_chunk(xc_ref, z_ref, dt_ref, dtb_ref, alog_ref, dsk_ref, gn_ref, rexp_ref, ltri_ref, y_ref, h_scr, r0)

    @pl.when(c == pl.num_programs(1) - 1)
    def _():
        hout_ref[0] = h_scr[...].T


def _ssd_chunk(xc_ref, z_ref, dt_ref, dtb_ref, alog_ref, dsk_ref, gn_ref, rexp_ref, ltri_ref, y_ref, h_scr, r0):
    chunk = SSD_CHUNK
    di = z_ref.shape[1]
    gn_w = SSD_GROUPS * SSD_D_STATE
    gw = di // SSD_GROUPS
    rs = slice(r0, r0 + chunk)

    xs = xc_ref[rs, :di].astype(f32)
    bm = xc_ref[rs, di:di + gn_w]
    cm = xc_ref[rs, di + gn_w:]

    dt = _softplus(dt_ref[rs, :] + dtb_ref[...])
    da = dt * (-jnp.exp(alog_ref[...]))
    acs = jnp.dot(ltri_ref[...], da, precision=lax.Precision.HIGHEST, preferred_element_type=f32)
    acs_t = acs.T
    dt_t = dt.T
    rexp = rexp_ref[...]
    eacs_e = _split_dot(jnp.exp(acs), rexp)
    xw = (xs * _split_dot(jnp.exp(acs[chunk - 1:chunk, :] - acs) * dt, rexp)).astype(bf16)

    row = lax.broadcasted_iota(jnp.int32, (chunk, chunk), 0)
    col = lax.broadcasted_iota(jnp.int32, (chunk, chunk), 1)
    causal = row >= col
    low_half = col < SSD_HEAD_DIM
    zero = jnp.zeros((chunk, LANES), bf16)
    heads_per_group = SSD_HEADS // SSD_GROUPS
    ys = []
    for g in range(SSD_GROUPS):
        ns = slice(g * SSD_D_STATE, (g + 1) * SSD_D_STATE)
        ls = slice(g * gw, (g + 1) * gw)
        cb = lax.dot_general(cm[:, ns], bm[:, ns], NT_DIMS, preferred_element_type=f32)
        hg = h_scr[:, ls]
        y_off = jnp.dot(cm[:, ns], hg.astype(bf16), preferred_element_type=f32) * eacs_e[:, ls]
        y_diag = []
        for pr in range(heads_per_group // 2):
            h0 = g * heads_per_group + 2 * pr
            xp = xc_ref[rs, h0 * SSD_HEAD_DIM:h0 * SSD_HEAD_DIM + LANES]
            ws = []
            for hh in (h0, h0 + 1):
                diff = acs[:, hh:hh + 1] - acs_t[hh:hh + 1, :]
                ws.append((cb * jnp.exp(jnp.where(causal, diff, MASKED)) * dt_t[hh:hh + 1, :]).astype(bf16))
            x2 = jnp.concatenate([jnp.where(low_half, xp, zero), jnp.where(low_half, zero, xp)], axis=0)
            y_diag.append(jnp.dot(jnp.concatenate(ws, axis=1), x2, preferred_element_type=f32))
        st = lax.dot_general(bm[:, ns], xw[:, ls], TN_DIMS, preferred_element_type=f32)
        h_scr[:, ls] = eacs_e[chunk - 1:chunk, ls] * hg + st
        ys.append(jnp.concatenate(y_diag, axis=-1) + y_off)
    y = jnp.concatenate(ys, axis=-1) + dsk_ref[...] * xs
    y_ref[rs, :] = _ssd_gate_norm(y, z_ref[rs, :].astype(f32), gn_ref[...]).astype(y_ref.dtype)


def _ssd_prompt(xc, z, dt, ssd_w, nbatch, t, chunks_per_step):
    _, _, dtb, alog, dsk, gn, rexp, ltri = ssd_w
    ch, di = xc.shape[1], z.shape[1]
    tr = chunks_per_step * SSD_CHUNK
    nstep = t // tr
    rows = lambda b, c: (b * nstep + c, 0)
    est = (2 * (_nbytes((tr, ch), bf16) + 2 * _nbytes((tr, di), bf16) + _nbytes((di, SSD_D_STATE), f32))
           + _nbytes((SSD_D_STATE, di), f32) + 24 * _nbytes((SSD_CHUNK, ch), f32))
    return pl.pallas_call(
        _ssd_prompt_body,
        out_shape=[SDS((nbatch * t, di), bf16), SDS((nbatch, di, SSD_D_STATE), f32)],
        grid=(nbatch, nstep),
        in_specs=[pl.BlockSpec((tr, ch), rows), pl.BlockSpec((tr, di), rows),
                  pl.BlockSpec((tr, LANES), rows),
                  _resident(dtb.shape), _resident(alog.shape),
                  _resident(dsk.shape), _resident(gn.shape), _resident(rexp.shape), _resident(ltri.shape)],
        out_specs=[pl.BlockSpec((tr, di), rows), pl.BlockSpec((1, di, SSD_D_STATE), lambda b, c: (b, 0, 0))],
        scratch_shapes=[pltpu.VMEM((SSD_D_STATE, di), f32)],
        compiler_params=_params(("parallel", "arbitrary"), est), name="ssd_prompt")(xc, z, dt, dtb, alog, dsk, gn, rexp, ltri)


def _ssd_step_body(xbc_ref, z_ref, dt_ref, cs_ref, h_ref, cw_ref, cb_ref, dtb_ref, alog_ref, dsk_ref, gn_ref, rexp_ref,
                   y_ref, cso_ref, ho_ref, xdt_scr, da_scr, bm_scr, cm_scr, y_scr, *, bb):
    ch = xbc_ref.shape[1]
    di = z_ref.shape[1]
    gn_w = SSD_GROUPS * SSD_D_STATE
    gw = di // SSD_GROUPS
    halo = SSD_CONV - 1
    xr = xbc_ref[...]
    conv = cb_ref[...] + cw_ref[halo:halo + 1, :] * xr
    for k in range(halo):
        conv = conv + cw_ref[k:k + 1, :] * cs_ref[:, k * ch:(k + 1) * ch]
    for k in range(1, halo):
        cso_ref[:, (k - 1) * ch:k * ch] = cs_ref[:, k * ch:(k + 1) * ch]
    cso_ref[:, (halo - 1) * ch:] = xr
    xc = _silu(conv)
    xs = xc[:, :di]
    dt = _softplus(dt_ref[...] + dtb_ref[...])
    rexp = rexp_ref[...]
    xdt_scr[...] = xs * _split_dot(dt, rexp)
    da_scr[...] = _split_dot(jnp.exp(dt * (-jnp.exp(alog_ref[...]))), rexp)
    bm_scr[...] = xc[:, di:di + gn_w]
    cm_scr[...] = xc[:, di + gn_w:]

    rows = 2 * SUBLANES
    rowi = lax.broadcasted_iota(jnp.int32, (rows, di), 0)
    grp = lax.broadcasted_iota(jnp.int32, (rows, di), 1) // gw
    rown = lax.broadcasted_iota(jnp.int32, (rows, SSD_D_STATE), 0)
    ones_rows = jnp.where((rown == SSD_GROUPS) | (rown == SSD_GROUPS + 1), 1.0, 0.0).astype(f32)

    def per_row(b, carry):
        xrow = xdt_scr[pl.ds(b, 1), :]
        drow = da_scr[pl.ds(b, 1), :]
        d_hi = drow.astype(bf16).astype(f32)
        lhs = jnp.where(rowi == grp, xrow, jnp.where(rowi == SSD_GROUPS, d_hi, jnp.where(rowi == SSD_GROUPS + 1, drow - d_hi, 0.0)))
        brow = bm_scr[pl.ds(b, 1), :]
        crow = cm_scr[pl.ds(b, 1), :]
        rhs_b = jnp.zeros((rows, SSD_D_STATE), f32)
        c_rows = jnp.zeros((rows, SSD_D_STATE), f32)
        for g in range(SSD_GROUPS):
            ns = slice(g * SSD_D_STATE, (g + 1) * SSD_D_STATE)
            rhs_b = jnp.where(rown == g, brow[:, ns], rhs_b)
            c_rows = jnp.where(rown == g, crow[:, ns], c_rows)
        rhs = jnp.concatenate([rhs_b, ones_rows], axis=1).astype(bf16)
        sd = lax.dot_general(lhs.astype(bf16), rhs, TN_DIMS, preferred_element_type=f32)
        hn = sd[:, SSD_D_STATE:] * h_ref[b] + sd[:, :SSD_D_STATE]
        ho_ref[b] = hn
        y8 = lax.dot_general(c_rows.astype(bf16), hn.astype(bf16), NT_DIMS, preferred_element_type=f32)
        y_scr[pl.ds(b, 1), :] = jnp.sum(jnp.where(rowi == grp, y8, 0.0), axis=0, keepdims=True)
        return carry

    lax.fori_loop(0, bb, per_row, 0, unroll=True)
    y = y_scr[...] + dsk_ref[...] * xs
    y_ref[...] = _ssd_gate_norm(y, _silu(z_ref[...]), gn_ref[...]).astype(y_ref.dtype)


def _ssd_step(xbc, z, dt, conv_state, h0, ssd_w, bb):
    cw, cb, dtb, alog, dsk, gn, rexp, _ = ssd_w
    nb, ch = xbc.shape
    di = z.shape[1]
    gn_w = SSD_GROUPS * SSD_D_STATE
    halo = SSD_CONV - 1
    est = (2 * (2 * _nbytes((bb, di, SSD_D_STATE), f32) + 2 * _nbytes((bb, halo * ch), f32) + 4 * _nbytes((bb, ch), f32))
           + 8 * _nbytes((di, 2 * SSD_D_STATE), f32))
    body = functools.partial(_ssd_step_body, bb=bb)
    r2 = lambda i: (i, 0)
    return pl.pallas_call(
        body,
        out_shape=[SDS((nb, di), bf16), SDS((nb, halo * ch), f32), SDS((nb, di, SSD_D_STATE), f32)],
        grid=(nb // bb,),
        in_specs=[pl.BlockSpec((bb, ch), r2), pl.BlockSpec((bb, di), r2), pl.BlockSpec((bb, LANES), r2),
                  pl.BlockSpec((bb, halo * ch), r2), pl.BlockSpec((bb, di, SSD_D_STATE), lambda i: (i, 0, 0)),
                  _resident(cw.shape), _resident(cb.shape), _resident(dtb.shape), _resident(alog.shape),
                  _resident(dsk.shape), _resident(gn.shape), _resident(rexp.shape)],
        out_specs=[pl.BlockSpec((bb, di), r2), pl.BlockSpec((bb, halo * ch), r2),
                   pl.BlockSpec((bb, di, SSD_D_STATE), lambda i: (i, 0, 0))],
        scratch_shapes=[pltpu.VMEM((bb, di), f32), pltpu.VMEM((bb, di), f32), pltpu.VMEM((bb, gn_w), f32),
                        pltpu.VMEM((bb, gn_w), f32), pltpu.VMEM((bb, di), f32)],
        compiler_params=_params(("parallel",), est), name="ssd_step")(xbc, z, dt, conv_state, h0, cw, cb, dtb, alog, dsk, gn, rexp)


def _merge_body(a_ref, s_ref, m_ref, g0_ref, g1_ref, g2_ref, wa_ref, ws_ref, wm_ref, o_ref, *, chunk):
    a, s, m = a_ref[...], s_ref[...], m_ref[...]
    for c0 in range(0, o_ref.shape[1], chunk):
        cs = slice(c0, c0 + chunk)
        acc = g0_ref[:, cs].astype(f32) * jnp.dot(a, wa_ref[:, cs], preferred_element_type=f32)
        acc = acc + g1_ref[:, cs].astype(f32) * jnp.dot(s, ws_ref[:, cs], preferred_element_type=f32)
        acc = acc + g2_ref[:, cs].astype(f32) * jnp.dot(m, wm_ref[:, cs], preferred_element_type=f32)
        o_ref[:, cs] = acc.astype(o_ref.dtype)


def _merge(a, s, mo, gates, wa, ws, wm, tm):
    m, d = s.shape[0], wa.shape[1]
    chunk = 512
    est = (2 * (_nbytes((tm, a.shape[1]), bf16) + _nbytes((tm, s.shape[1]), bf16) + _nbytes((tm, mo.shape[1]), bf16)
                + 4 * _nbytes((tm, d), bf16))
           + _nbytes(wa.shape, bf16) + _nbytes(ws.shape, bf16) + _nbytes(wm.shape, bf16) + 6 * _nbytes((tm, chunk), f32))
    body = functools.partial(_merge_body, chunk=chunk)
    return pl.pallas_call(
        body, out_shape=SDS((m, d), bf16), grid=(m // tm,),
        in_specs=[pl.BlockSpec((tm, a.shape[1]), lambda i: (i, 0)), pl.BlockSpec((tm, s.shape[1]), lambda i: (i, 0)),
                  pl.BlockSpec((tm, mo.shape[1]), lambda i: (i, 0)),
                  pl.BlockSpec((tm, d), lambda i: (i, 0)), pl.BlockSpec((tm, d), lambda i: (i, 1)),
                  pl.BlockSpec((tm, d), lambda i: (i, 2)),
                  _resident(wa.shape), _resident(ws.shape), _resident(wm.shape)],
        out_specs=pl.BlockSpec((tm, d), lambda i: (i, 0)),
        compiler_params=_params(("parallel",), est), name="merge")(a, s, mo, gates, gates, gates, wa, ws, wm)


def _out_proj_body(m_ref, w_ref, x_ref, g_ref, x1_ref, h2_ref, *, chunk):
    m = m_ref[...]
    for c0 in range(0, x1_ref.shape[1], chunk):
        cs = slice(c0, c0 + chunk)
        x1_ref[:, cs] = x_ref[:, cs] + jnp.dot(m, w_ref[:, cs], preferred_element_type=f32)
    x1 = x1_ref[...]
    ms = jnp.mean(x1 * x1, axis=-1, keepdims=True)
    h2_ref[...] = (x1 * lax.rsqrt(ms + EPS) * g_ref[...]).astype(h2_ref.dtype)


def _out_proj(merged, w, x, gain, tm):
    m, d = x.shape
    rows = pl.BlockSpec((tm, d), lambda i: (i, 0))
    est = 2 * (_nbytes((tm, d), bf16) + 2 * _nbytes((tm, d), f32) + _nbytes((tm, d), bf16)) + _nbytes(w.shape, bf16) + (4 << 20)
    body = functools.partial(_out_proj_body, chunk=512)
    return pl.pallas_call(
        body, out_shape=[SDS((m, d), f32), SDS((m, d), bf16)], grid=(m // tm,),
        in_specs=[rows, _resident(w.shape), rows, _resident((1, d))], out_specs=[rows, rows],
        compiler_params=_params(("parallel",), est), name="out_proj")(merged, w, x, gain.reshape(1, d))


def _ffn_act_body(h_ref, wg_ref, wu_ref, a_ref, *, chunk):
    h = h_ref[...]
    for s0 in range(0, a_ref.shape[1], chunk):
        cs = slice(s0, s0 + chunk)
        gate = jnp.dot(h, wg_ref[:, cs], preferred_element_type=f32)
        up = jnp.dot(h, wu_ref[:, cs], preferred_element_type=f32)
        a_ref[:, cs] = (_silu(gate) * up).astype(a_ref.dtype)


def _ffn_act(h2, wg, wu, tm, tf):
    m, d = h2.shape
    dff = wg.shape[1]
    w_spec = pl.BlockSpec((d, tf), lambda c, i: (0, c), pipeline_mode=pl.Buffered(1))
    est = 2 * _nbytes((d, tf), bf16) + 2 * (_nbytes((tm, d), bf16) + _nbytes((tm, tf), bf16)) + (6 << 20)
    body = functools.partial(_ffn_act_body, chunk=256)
    return pl.pallas_call(
        body, out_shape=SDS((m, dff), bf16), grid=(dff // tf, m // tm),
        in_specs=[pl.BlockSpec((tm, d), lambda c, i: (i, 0)), w_spec, w_spec],
        out_specs=pl.BlockSpec((tm, tf), lambda c, i: (i, c)),
        compiler_params=_params(("arbitrary", "arbitrary"), est), name="ffn_act")(h2, wg, wu)


def _ffn_down_body(a_ref, w_ref, x_ref, o_ref, *, chunk):
    a = a_ref[...]
    for c0 in range(0, o_ref.shape[1], chunk):
        cs = slice(c0, c0 + chunk)
        o_ref[:, cs] = x_ref[:, cs] + jnp.dot(a, w_ref[:, cs], preferred_element_type=f32)


def _ffn_down(act, wd, x1, tm):
    m, d = x1.shape
    dff = act.shape[1]
    rows = pl.BlockSpec((tm, d), lambda i: (i, 0))
    est = 2 * (_nbytes((tm, dff), bf16) + 2 * _nbytes((tm, d), f32)) + _nbytes(wd.shape, bf16) + (2 << 20)
    body = functools.partial(_ffn_down_body, chunk=512)
    return pl.pallas_call(
        body, out_shape=SDS((m, d), f32), grid=(m // tm,),
        in_specs=[pl.BlockSpec((tm, dff), lambda i: (i, 0)), _resident(wd.shape), rows], out_specs=rows,
        compiler_params=_params(("parallel",), est), name="ffn_down")(act, wd, x1)


def _dup_heads(w):
    lead = w.shape[:-1]
    w = w.reshape(*lead, SWA_KV_HEADS, 1, SWA_HEAD_DIM)
    return jnp.broadcast_to(w, (*lead, SWA_KV_HEADS, 2, SWA_HEAD_DIM)).reshape(*lead, SWA_KV_HEADS * LANES)


def _pad_lanes(v, n=LANES):
    return jnp.pad(v, [(0, 0)] * (v.ndim - 1) + [(0, n - v.shape[-1])])


def _layer(xp, xs, cache_k, cache_v, cmem_k, cmem_v, state_ssm, state_conv, mem_prompt, w):
    bp, tp, d = xp.shape
    bs = xs.shape[0]
    w_buf = cache_k.shape[1]
    assert xs.shape[1] == 1 and w_buf == WINDOW and tp % SSD_CHUNK == 0 and bs % SUBLANES == 0
    q_dim = SWA_HEADS * SWA_HEAD_DIM
    kv_dim = SWA_KV_HEADS * SWA_HEAD_DIM
    di = SSD_HEADS * SSD_HEAD_DIM
    ch = di + 2 * SSD_GROUPS * SSD_D_STATE
    md = MEM_HEADS * MEM_HEAD_DIM
    o = 0
    w_in = w["w_in"].astype(bf16)
    wq, o = w_in[:, o:o + q_dim], o + q_dim
    wk, o = w_in[:, o:o + kv_dim], o + kv_dim
    wv, o = w_in[:, o:o + kv_dim], o + kv_dim
    wz, o = w_in[:, o:o + di], o + di
    wxbc, o = w_in[:, o:o + ch], o + ch
    wdt, o = w_in[:, o:o + SSD_HEADS], o + SSD_HEADS
    wqm, o = w_in[:, o:o + md], o + md
    wg = w_in[:, o:]
    wdt = _pad_lanes(wdt)

    q_scale = SWA_HEAD_DIM ** -0.5
    gq = jnp.tile(w["q_norm_swa"], SWA_HEADS) * q_scale
    gk = jnp.tile(w["k_norm_swa"], SWA_KV_HEADS)
    head_of = np.arange(2 * LANES) // SWA_HEAD_DIM
    blockdiag = jnp.asarray(head_of[:, None] == head_of[None, :], bf16)
    w_qkv_p = jnp.concatenate([wq, _dup_heads(wk), _dup_heads(wv), wdt], axis=1)
    w_qkv_s = jnp.concatenate([wq, wk, wv, wdt], axis=1)
    gqm = (jnp.tile(w["q_norm_mem"], MEM_HEADS) * MEM_HEAD_DIM ** -0.5).reshape(1, md)
    gkm = jnp.tile(w["k_norm_mem"], MEM_HEADS).reshape(1, md)
    wz_b, wxbc_b, wqm_b, wg_b = wz, wxbc, wqm, wg
    w_mem_k, w_mem_v = w["w_mem_kv"][:, :md].astype(bf16), w["w_mem_kv"][:, md:].astype(bf16)
    wa, ws, wm = w["w_up_swa"].astype(bf16), w["w_up_ssd"].astype(bf16), w["w_up_mem"].astype(bf16)
    w_out = w["w_out"].astype(bf16)
    w_gate, w_up, w_down = w["w_gate"].astype(bf16), w["w_up"].astype(bf16), w["w_down"].astype(bf16)

    head_of_lane = np.arange(di) // SSD_HEAD_DIM
    rexp = jnp.asarray(np.arange(LANES)[:, None] == head_of_lane[None, :], bf16)
    ltri = jnp.asarray(np.arange(SSD_CHUNK)[:, None] >= np.arange(SSD_CHUNK)[None, :], f32)
    ssd_w = (w["conv_w"], w["conv_b"].reshape(1, ch), _pad_lanes(w["dt_bias"].reshape(1, -1)),
             _pad_lanes(w["a_log"].reshape(1, -1)), jnp.repeat(w["d_skip"], SSD_HEAD_DIM).reshape(1, di),
             w["ssd_norm"].reshape(1, di), rexp, ltri)
    slopes = jnp.asarray(np.broadcast_to(np.exp2(-8.0 * np.arange(1, SWA_HEADS + 1) / SWA_HEADS)[:, None], (SWA_HEADS, LANES)), f32)
    sinks_b = jnp.broadcast_to(w["swa_sinks"].astype(f32)[:, None], (SWA_HEADS, LANES))

    dff = w_gate.shape[1]
    tf = dff // 2 if dff % (4 * LANES) == 0 else dff

    def finish(x2, a_out, s_out, m_out, gates, tm, tm_act):
        merged = _merge(a_out, s_out, m_out, gates, wa, ws, wm, tm)
        x1, h2 = _out_proj(merged, w_out, x2, w["norm_ffn"], tm)
        act = _ffn_act(h2, w_gate, w_up, tm_act, tf)
        return _ffn_down(act, w_down, x1, tm)

    mp = bp * tp
    x2 = xp.reshape(mp, d)
    tm = min(1024, tp)
    tpb = tp // tm
    tm_qkv = min(512, tp)
    g_mix = w["norm_mix"].reshape(1, d)
    h, q, kd, v4, dt, klast, vlast = _qkv_proj(
        x2, g_mix, w_qkv_p, (gq * LOG2_E).reshape(1, -1), _dup_heads(gk).reshape(1, -1), blockdiag,
        q_dim, 2 * kv_dim, 2 * kv_dim, tm_qkv, tp // tm_qkv, bf16, bf16, True)
    z = _proj(h, wz_b, _epi_silu, bf16, tm, di, name="z_proj")
    xc, xtail = _xbc_proj(h, wxbc_b, ssd_w[0], ssd_w[1], tm, tpb)
    qm = _proj(h, wqm_b, _epi_headnorm256, bf16, tm, md, aux=(gqm,), aux_specs=(_resident((1, md)),), name="qm_proj")
    gates = _proj(h, wg_b, _epi_sigmoid, bf16, tm, d, name="gate_proj")
    a_out = _swa_prompt(q, kd, v4, w["swa_sinks"].astype(f32) * LOG2_E, bp, tp)
    s_out, p_h = _ssd_prompt(xc, z, dt, ssd_w, bp, tp, 4 if tp % (4 * SSD_CHUNK) == 0 else 1)
    mt = mem_prompt.shape[1]
    hm = _rmsnorm(mem_prompt.reshape(bp * mt, d), w["norm_mem"], mt)
    mk = _proj(hm, w_mem_k, _epi_headnorm256, f32, mt, md, aux=(gkm,), aux_specs=(_resident((1, md)),), name="mem_k_proj")
    mv = _proj(hm, w_mem_v, _epi_plain, f32, mt, md, name="mem_v_proj")
    m_out = _mem_attn(qm.reshape(bp, tp, md), mk.reshape(bp, mt, md), mv.reshape(bp, mt, md), 1, tm)
    yp = finish(x2, a_out, s_out, m_out.reshape(mp, md), gates, min(512, tp), tm).reshape(bp, tp, d)

    undup = lambda t: t.reshape(bp, WINDOW, SWA_KV_HEADS, 2, SWA_HEAD_DIM)[:, :, :, 0, :]
    p_k, p_v = undup(klast), undup(vlast)
    p_mk = mk.reshape(bp, mt, MEM_HEADS, MEM_HEAD_DIM)
    p_mv = mv.reshape(bp, mt, MEM_HEADS, MEM_HEAD_DIM)
    p_h = p_h.reshape(bp, SSD_HEADS, SSD_HEAD_DIM, SSD_D_STATE)
    p_c = xtail.reshape(bp, SUBLANES, ch)[:, SUBLANES - (SSD_CONV - 1):, :]

    xs2 = xs.reshape(bs, d)
    hs, qs, ks, vs, dts = _qkv_proj(xs2, g_mix, w_qkv_s, gq.reshape(1, -1), gk.reshape(1, -1), blockdiag,
                                    q_dim, kv_dim, kv_dim, bs, 1, f32, f32, False)
    zs = _proj(hs, wz_b, _epi_plain, f32, bs, di, name="z_proj_s")
    xbcs = _proj(hs, wxbc_b, _epi_plain, f32, bs, ch, name="xbc_proj_s")
    qms = _proj(hs, wqm_b, _epi_headnorm256, bf16, bs, md, aux=(gqm,), aux_specs=(_resident((1, md)),), name="qm_proj_s")
    gates_s = _proj(hs, wg_b, _epi_sigmoid, bf16, bs, d, name="gate_proj_s")
    own_kv = jnp.asarray((np.arange(SWA_HEADS) // SWA_GROUP)[:, None] == np.arange(SWA_KV_HEADS)[None, :], f32)
    q_blk = (qs.reshape(bs, SWA_HEADS, 1, SWA_HEAD_DIM) * own_kv[None, :, :, None]).reshape(bs, SWA_HEADS, kv_dim)
    a_blk, s_kt, s_vt = _swa_decode(q_blk, ks, vs, jnp.transpose(cache_k, (0, 2, 3, 1)), jnp.transpose(cache_v, (0, 2, 3, 1)),
                                    slopes, sinks_b, SUBLANES)
    a_s = jnp.sum(a_blk.reshape(bs, SWA_HEADS, SWA_KV_HEADS, SWA_HEAD_DIM), axis=2).astype(bf16)
    s_k, s_v = jnp.transpose(s_kt, (0, 3, 1, 2)), jnp.transpose(s_vt, (0, 3, 1, 2))
    halo = SSD_CONV - 1
    s_s, s_c, s_h = _ssd_step(xbcs, zs, dts, state_conv.reshape(bs, halo * ch), state_ssm.reshape(bs, di, SSD_D_STATE),
                              ssd_w, SUBLANES)
    qm_rows = jnp.tile(qms.reshape(bs, MEM_HEADS, MEM_HEAD_DIM), (1, 2 * SUBLANES // MEM_HEADS, 1))
    m_s = _mem_decode(qm_rows, cmem_k, cmem_v, 4)[:, :MEM_HEADS, :].reshape(bs, md)
    ys = finish(xs2, a_s.reshape(bs, q_dim), s_s, m_s, gates_s, bs, bs).reshape(bs, 1, d)

    s_h =s_h.reshape(bs, SSD_HEADS, SSD_HEAD_DIM, SSD_D_STATE)
    s_c = s_c.reshape(bs, halo, ch)
    return yp, ys, (p_k, p_v, p_mk, p_mv, p_h, p_c), (s_k, s_v, s_h, s_c)


_WEIGHT_NAMES = ("norm_mix", "w_in", "q_norm_swa", "k_norm_swa", "swa_sinks", "conv_w", "conv_b", "dt_bias", "a_log",
                 "d_skip", "ssd_norm", "norm_mem", "w_mem_kv", "q_norm_mem", "k_norm_mem", "w_up_swa", "w_up_ssd",
                 "w_up_mem", "w_out", "norm_ffn", "w_gate", "w_up", "w_down")


def kernel(x_prompt, x_sample, cache_swa_k, cache_swa_v, cache_mem_k, cache_mem_v, state_ssm, state_conv, mem_prompt, norm_mix, w_in, q_norm_swa, k_norm_swa, swa_sinks, conv_w, conv_b, dt_bias, a_log, d_skip, ssd_norm, norm_mem, w_mem_kv, q_norm_mem, k_norm_mem, w_up_swa, w_up_ssd, w_up_mem, w_out, norm_ffn, w_gate, w_up, w_down):
    weights = (norm_mix, w_in, q_norm_swa, k_norm_swa, swa_sinks, conv_w, conv_b, dt_bias, a_log, d_skip, ssd_norm,
               norm_mem, w_mem_kv, q_norm_mem, k_norm_mem, w_up_swa, w_up_ssd, w_up_mem, w_out, norm_ffn, w_gate, w_up, w_down)
    depth = w_in.shape[0]
    layer = (lambda a, l: a.reshape(a.shape[1:])) if depth == 1 else (lambda a, l: a[l])
    yp, ys = x_prompt, x_sample
    p_outs, s_outs = [], []
    for l in range(depth):
        w = {n: layer(a, l) for n, a in zip(_WEIGHT_NAMES, weights)}
        yp, ys, po, so = _layer(yp, ys, layer(cache_swa_k, l), layer(cache_swa_v, l), layer(cache_mem_k, l),
                                layer(cache_mem_v, l), layer(state_ssm, l), layer(state_conv, l), mem_prompt, w)
        p_outs.append(po)
        s_outs.append(so)
    stack = lambda outs, i: jnp.stack([o[i] for o in outs])
    return (yp, ys, *(stack(p_outs, i) for i in range(6)), *(stack(s_outs, i) for i in range(4)))
```
